```python
import math
import jax, jax.numpy as jnp
from jax import lax
import numpy as np

D_MODEL = 1024
BATCH = 16
SEQ = 256
DEPTH = 4
DEC_BATCH = 4
DEC_SEQ = 2048
PAST_LEN = 256

GRID_W = 64
MIX_WIDTH = D_MODEL
GROUP_W = MIX_WIDTH // 4
A_HEAD_DIM = 64
A_Q_HEADS = GROUP_W // A_HEAD_DIM
A_KV_HEADS = A_Q_HEADS // 2
B_HEADS = 4
B_V_DIM = GROUP_W // B_HEADS
B_QK_DIM = B_V_DIM // 2
POOL_WINDOWS = (2, 4, 8, 16)
C_GROUPS = len(POOL_WINDOWS)
C_GROUP_DIM = GROUP_W // C_GROUPS
CHUNK = 128
D_GROUPS = 4
D_GROUP_DIM = GROUP_W // D_GROUPS
D_FF = -(-8 * D_MODEL // (3 * 256)) * 256
N_MOD = 6
QBLOCK = 128
RMS_EPS = 1e-6
ROPE_THETA = 10000.0

IN_SIZES = [A_Q_HEADS * A_HEAD_DIM, A_KV_HEADS * A_HEAD_DIM, A_KV_HEADS * A_HEAD_DIM,
            B_HEADS * 2 * B_QK_DIM, B_HEADS * 2 * B_QK_DIM, B_HEADS * B_V_DIM,
            GROUP_W, 2 * GROUP_W]
IN_WIDTH = sum(IN_SIZES)
IN_SPLITS = [int(v) for v in np.cumsum(IN_SIZES)[:-1]]

kernel_name = "hybrid_diffusion_prefix_step"


def rms_norm(x, g):
    xf = x.astype(jnp.float32)
    y = xf * lax.rsqrt(jnp.mean(xf * xf, axis=-1, keepdims=True) + RMS_EPS)
    return (y * g.astype(jnp.float32)).astype(x.dtype)


def axial_rope(x):
    s, d = x.shape[1], x.shape[-1]
    rows = s // GRID_W
    row = jnp.repeat(jnp.arange(rows), GRID_W)
    col = jnp.tile(jnp.arange(GRID_W), rows)
    half = d // 2
    inv_freq = ROPE_THETA ** (-jnp.arange(0, half, 2, dtype=jnp.float32) / half)

    def rotate(xp, pos):
        ang = pos.astype(jnp.float32)[:, None] * inv_freq[None, :]
        ang = jnp.concatenate([ang, ang], axis=-1)[None, :, None, :]
        xf = xp.astype(jnp.float32)
        x1, x2 = jnp.split(xf, 2, axis=-1)
        return xf * jnp.cos(ang) + jnp.concatenate([-x2, x1], axis=-1) * jnp.sin(ang)

    out = jnp.concatenate([rotate(x[..., :half], row), rotate(x[..., half:], col)], axis=-1)
    return out.astype(x.dtype)


def gqa_attention(q, k, v):
    b, sq, hq, d = q.shape
    hkv = k.shape[2]
    g = hq // hkv
    nb = sq // QBLOCK
    scale = d ** -0.5
    qb = jnp.moveaxis(q.reshape(b, nb, QBLOCK, hkv, g, d), 1, 0)

    def block(qi):
        s = jnp.einsum("bqhgd,bkhd->bhgqk", qi, k, preferred_element_type=jnp.float32) * scale
        p = jax.nn.softmax(s, axis=-1)
        return jnp.einsum("bhgqk,bkhd->bqhgd", p.astype(v.dtype), v)

    o = jnp.moveaxis(lax.map(block, qb), 0, 1)
    return o.reshape(b, sq, hq * d)


def diff_attention(q, k, v, lam, lam_init, g_subln):
    b, sq, h, _, d = q.shape
    e = v.shape[-1]
    nb = sq // QBLOCK
    scale = d ** -0.5
    qb = jnp.moveaxis(q.reshape(b, nb, QBLOCK, h, 2, d), 1, 0)

    def block(qi):
        s = jnp.einsum("bqhjd,bkhjd->bhjqk", qi, k, preferred_element_type=jnp.float32) * scale
        p = jax.nn.softmax(s, axis=-1)
        w = p[:, :, 0] - lam * p[:, :, 1]
        return jnp.einsum("bhqk,bkhe->bqhe", w.astype(v.dtype), v)

    o = jnp.moveaxis(lax.map(block, qb), 0, 1).reshape(b, sq, h, e)
    o = rms_norm(o, g_subln) * (1.0 - lam_init)
    return o.reshape(b, sq, h * e)


def multiscale_pool(x, w_c, c_scale):
    b, s, _ = x.shape
    xg = x.reshape(b, s, C_GROUPS, C_GROUP_DIM)
    xf = xg.astype(jnp.float32)
    csum = jnp.concatenate([jnp.zeros((b, 1, C_GROUPS, C_GROUP_DIM), jnp.float32),
                            jnp.cumsum(xf, axis=1)], axis=1)
    t = jnp.arange(s)
    outs = []
    for gi, win in enumerate(POOL_WINDOWS):
        lo = jnp.clip(t - win // 2, 0, s)
        hi = jnp.clip(t + win // 2, 0, s)
        cs = csum[:, :, gi]
        mean = (cs[:, hi] - cs[:, lo]) / (hi - lo).astype(jnp.float32)[None, :, None]
        outs.append(mean - xf[:, :, gi])
    pooled = jnp.stack(outs, axis=2).astype(x.dtype)
    y = jnp.einsum("bsgc,gce->bsge", pooled, w_c).reshape(b, s, GROUP_W)
    return y * c_scale


def spatial_gating(z, g_v, w_s, b_s):
    b, s, _ = z.shape
    z = jax.nn.gelu(z)
    u, v = jnp.split(z, 2, axis=-1)
    v = rms_norm(v, g_v)
    vg = v.reshape(b, s // CHUNK, CHUNK, D_GROUPS, D_GROUP_DIM)
    sg = jnp.einsum("gpq,bnqgc->bnpgc", w_s, vg) + b_s.T[None, None, :, :, None]
    return u * sg.reshape(b, s, GROUP_W)


def modulation(cond, w_mod, b_mod):
    m = jax.nn.silu(cond) @ w_mod + b_mod
    return m.reshape(cond.shape[:-1] + (N_MOD, D_MODEL))


def mixer_block(h, p, lam_init, ctx):
    b, s, _ = h.shape
    proj = h @ p["w_in"]
    aq, ak, av, bq, bk, bv, cx, dz = jnp.split(proj, IN_SPLITS, axis=-1)
    aq = rms_norm(aq.reshape(b, s, A_Q_HEADS, A_HEAD_DIM), p["a_q_norm"])
    ak = rms_norm(ak.reshape(b, s, A_KV_HEADS, A_HEAD_DIM), p["a_k_norm"])
    av = av.reshape(b, s, A_KV_HEADS, A_HEAD_DIM)
    bq = bq.reshape(b, s, B_HEADS, 2, B_QK_DIM)
    bk = bk.reshape(b, s, B_HEADS, 2, B_QK_DIM)
    bv = bv.reshape(b, s, B_HEADS, B_V_DIM)
    own = (ak, av, bk, bv)
    if ctx is not None:
        aq = axial_rope(aq)
        ak = jnp.concatenate([ctx[0], axial_rope(ak)], axis=1)
        av = jnp.concatenate([ctx[1], av], axis=1)
        bq = axial_rope(bq.reshape(b, s, B_HEADS * 2, B_QK_DIM)).reshape(b, s, B_HEADS, 2, B_QK_DIM)
        bk_lat = axial_rope(bk.reshape(b, s, B_HEADS * 2, B_QK_DIM)).reshape(b, s, B_HEADS, 2, B_QK_DIM)
        bk = jnp.concatenate([ctx[2], bk_lat], axis=1)
        bv = jnp.concatenate([ctx[3], bv], axis=1)
    a_out = gqa_attention(aq, ak, av)
    f32 = jnp.float32
    lam = (jnp.exp(jnp.sum(p["b_lq1"].astype(f32) * p["b_lk1"].astype(f32)))
           - jnp.exp(jnp.sum(p["b_lq2"].astype(f32) * p["b_lk2"].astype(f32))) + lam_init)
    b_out = diff_attention(bq, bk, bv, lam, lam_init, p["b_subln"])
    c_out = multiscale_pool(cx, p["c_w"], p["c_scale"])
    d_out = spatial_gating(dz, p["d_v_norm"], p["d_ws"], p["d_bs"])
    y = jnp.concatenate([a_out.astype(h.dtype), b_out.astype(h.dtype),
                         c_out.astype(h.dtype), d_out.astype(h.dtype)], axis=-1) @ p["w_out"]
    return y, own


def trunk_layer(x, mod, p, lam_init, ctx):
    shift1, scale1, gate1, shift2, scale2, gate2 = [mod[:, i][:, None] for i in range(N_MOD)]
    h = rms_norm(x, p["g_pre_mix"]) * (1.0 + scale1) + shift1
    m, own = mixer_block(h, p, lam_init, ctx)
    x = x + gate1 * rms_norm(m, p["g_post_mix"])
    h = rms_norm(x, p["g_pre_ffn"]) * (1.0 + scale2) + shift2
    f = (jax.nn.silu(h @ p["w_gate"]) * (h @ p["w_up"])) @ p["w_down"]
    x = x + gate2 * rms_norm(f, p["g_post_ffn"])
    return x, own


def setup_inputs(seed: int = 0) -> dict:
    key = jax.random.key(seed)
    ks = jax.random.split(key, 32)
    f32 = jnp.float32

    def nrm(k, shape, s=1.0):
        return jax.random.normal(k, shape, f32) * s

    return {
        "x_prompt": nrm(ks[0], (BATCH, SEQ, D_MODEL)),
        "x_sample": nrm(ks[1], (DEC_BATCH, DEC_SEQ, D_MODEL)),
        "cache_a_k": nrm(ks[2], (DEC_BATCH, DEPTH, PAST_LEN, A_KV_HEADS, A_HEAD_DIM)),
        "cache_a_v": nrm(ks[3], (DEC_BATCH, DEPTH, PAST_LEN, A_KV_HEADS, A_HEAD_DIM)),
        "cache_b_k": nrm(ks[4], (DEC_BATCH, DEPTH, PAST_LEN, B_HEADS, 2, B_QK_DIM)),
        "cache_b_v": nrm(ks[5], (DEC_BATCH, DEPTH, PAST_LEN, B_HEADS, B_V_DIM)),
        "c": nrm(ks[6], (DEC_BATCH, D_MODEL)),
        "c_ctx": nrm(ks[7], (D_MODEL,)),
        "w_mod": nrm(ks[8], (DEPTH, D_MODEL, N_MOD * D_MODEL), 0.5 * D_MODEL ** -0.5),
        "b_mod": nrm(ks[9], (DEPTH, N_MOD * D_MODEL), 0.02),
        "g_pre_mix": 1.0 + nrm(ks[10], (DEPTH, D_MODEL), 0.05),
        "g_post_mix": 1.0 + nrm(ks[11], (DEPTH, D_MODEL), 0.05),
        "g_pre_ffn": 1.0 + nrm(ks[12], (DEPTH, D_MODEL), 0.05),
        "g_post_ffn": 1.0 + nrm(ks[13], (DEPTH, D_MODEL), 0.05),
        "w_in": nrm(ks[14], (DEPTH, D_MODEL, IN_WIDTH), D_MODEL ** -0.5),
        "w_out": nrm(ks[15], (DEPTH, MIX_WIDTH, D_MODEL), MIX_WIDTH ** -0.5),
        "a_q_norm": 1.0 + nrm(ks[16], (DEPTH, A_HEAD_DIM), 0.05),
        "a_k_norm": 1.0 + nrm(ks[17], (DEPTH, A_HEAD_DIM), 0.05),
        "b_lq1": nrm(ks[18], (DEPTH, B_QK_DIM), 0.1),
        "b_lk1": nrm(ks[19], (DEPTH, B_QK_DIM), 0.1),
        "b_lq2": nrm(ks[20], (DEPTH, B_QK_DIM), 0.1),
        "b_lk2": nrm(ks[21], (DEPTH, B_QK_DIM), 0.1),
        "b_subln": 1.0 + nrm(ks[22], (DEPTH, B_V_DIM), 0.05),
        "c_w": nrm(ks[23], (DEPTH, C_GROUPS, C_GROUP_DIM, C_GROUP_DIM), C_GROUP_DIM ** -0.5),
        "c_scale": 1.0 + nrm(ks[24], (DEPTH, GROUP_W), 0.1),
        "d_v_norm": 1.0 + nrm(ks[25], (DEPTH, GROUP_W), 0.05),
        "d_ws": nrm(ks[26], (DEPTH, D_GROUPS, CHUNK, CHUNK), CHUNK ** -0.5),
        "d_bs": 1.0 + nrm(ks[27], (DEPTH, D_GROUPS, CHUNK), 0.1),
        "w_gate": nrm(ks[28], (DEPTH, D_MODEL, D_FF), D_MODEL ** -0.5),
        "w_up": nrm(ks[29], (DEPTH, D_MODEL, D_FF), D_MODEL ** -0.5),
        "w_down": nrm(ks[30], (DEPTH, D_FF, D_MODEL), D_FF ** -0.5),
    }


def reference(x_prompt, x_sample, cache_a_k, cache_a_v, cache_b_k, cache_b_v, c, c_ctx,
              w_mod, b_mod, g_pre_mix, g_post_mix, g_pre_ffn, g_post_ffn, w_in, w_out,
              a_q_norm, a_k_norm, b_lq1, b_lk1, b_lq2, b_lk2, b_subln, c_w, c_scale,
              d_v_norm, d_ws, d_bs, w_gate, w_up, w_down):
    yp = x_prompt
    ys = x_sample
    new_ak, new_av, new_bk, new_bv = [], [], [], []
    for layer in range(DEPTH):
        p = {
            "w_in": w_in[layer], "w_out": w_out[layer],
            "g_pre_mix": g_pre_mix[layer], "g_post_mix": g_post_mix[layer],
            "g_pre_ffn": g_pre_ffn[layer], "g_post_ffn": g_post_ffn[layer],
            "a_q_norm": a_q_norm[layer], "a_k_norm": a_k_norm[layer],
            "b_lq1": b_lq1[layer], "b_lk1": b_lk1[layer],
            "b_lq2": b_lq2[layer], "b_lk2": b_lk2[layer], "b_subln": b_subln[layer],
            "c_w": c_w[layer], "c_scale": c_scale[layer],
            "d_v_norm": d_v_norm[layer], "d_ws": d_ws[layer], "d_bs": d_bs[layer],
            "w_gate": w_gate[layer], "w_up": w_up[layer], "w_down": w_down[layer],
        }
        lam_init = 0.8 - 0.6 * math.exp(-0.3 * layer)
        mod_ctx = modulation(c_ctx, w_mod[layer], b_mod[layer])[None]
        yp, own = trunk_layer(yp, mod_ctx, p, lam_init, None)
        new_ak.append(own[0])
        new_av.append(own[1])
        new_bk.append(own[2])
        new_bv.append(own[3])
        mod_lat = modulation(c, w_mod[layer], b_mod[layer])
        ctx = (cache_a_k[:, layer], cache_a_v[:, layer], cache_b_k[:, layer], cache_b_v[:, layer])
        ys, _ = trunk_layer(ys, mod_lat, p, lam_init, ctx)
    new_a_k = jnp.stack(new_ak, axis=1)
    new_a_v = jnp.stack(new_av, axis=1)
    new_b_k = jnp.stack(new_bk, axis=1)
    new_b_v = jnp.stack(new_bv, axis=1)
    return (yp, ys, new_a_k, new_a_v, new_b_k, new_b_v)
```

```python
import functools
import math

import numpy as np
import jax
import jax.numpy as jnp
from jax import lax
from jax.experimental import pallas as pl
from jax.experimental.pallas import tpu as pltpu

F32 = jnp.float32
BF16 = jnp.bfloat16

D_MODEL = 1024
DEPTH = 4
GRID_W = 64
GROUP_W = 256
A_HEAD_DIM = 64
B_QK_DIM = 32
POOL_WINDOWS = (2, 4, 8, 16)
CHUNK = 128
D_FF = 2816
N_MOD = 6
RMS_EPS = 1e-6
ROPE_THETA = 10000.0
IN_WIDTH = 2048

LANES = 128
SUBLANES = 8
VMEM_LIMIT_BYTES = 56 * 1024 * 1024

COL_AQ, COL_AK, COL_AV, COL_BQ, COL_BK, COL_BV, COL_DZ, COL_CX = 0, 256, 384, 512, 768, 1024, 1280, 1792
QKV_WIDTH = 1280
CD_WIDTH = 768
FF_CHUNK = 256
N_FF_CHUNKS = D_FF // FF_CHUNK

TOKEN_TILE = 512
Q_TILE = 256
MOD_COL_TILE = 1536
HALO = 8


def _rms(x, g):
    ms = jnp.mean(x * x, axis=-1, keepdims=True)
    return x * lax.rsqrt(ms + RMS_EPS) * g


def _dot(a, b):
    return jnp.dot(a, b, preferred_element_type=F32)


def _dot_nt(a, b):
    return lax.dot_general(a, b, (((1,), (1,)), ((), ())), preferred_element_type=F32)


def _split_dot(x, w):
    hi = x.astype(BF16)
    lo = (x - hi.astype(F32)).astype(BF16)
    return _dot(hi, w) + _dot(lo, w)


def _mod_kernel(cond_ref, w_ref, b_ref, out_ref):
    s = jax.nn.silu(cond_ref[...]).astype(BF16)
    out_ref[...] = _dot(s, w_ref[...].astype(BF16)) + b_ref[...]


def _modulation(cond8, w_mod, b_mod):
    n_col = (N_MOD * D_MODEL) // MOD_COL_TILE
    out = pl.pallas_call(
        _mod_kernel,
        grid=(DEPTH, n_col),
        in_specs=[
            pl.BlockSpec((SUBLANES, D_MODEL), lambda l, j: (0, 0)),
            pl.BlockSpec((None, D_MODEL, MOD_COL_TILE), lambda l, j: (l, 0, j)),
            pl.BlockSpec((None, 1, MOD_COL_TILE), lambda l, j: (l, 0, j)),
        ],
        out_specs=pl.BlockSpec((None, SUBLANES, MOD_COL_TILE), lambda l, j: (l, 0, j)),
        out_shape=jax.ShapeDtypeStruct((DEPTH, SUBLANES, N_MOD * D_MODEL), F32),
        compiler_params=pltpu.CompilerParams(
            dimension_semantics=("arbitrary", "arbitrary"), vmem_limit_bytes=VMEM_LIMIT_BYTES),
        name="modulation",
    )(cond8, w_mod, b_mod.reshape(DEPTH, 1, N_MOD * D_MODEL))
    return out.reshape(DEPTH, SUBLANES, N_MOD, D_MODEL)


def _rope_cols(x, cos, sin, shift, lo_mask):
    fwd = pltpu.roll(x, LANES - shift, axis=1)
    bwd = pltpu.roll(x, shift, axis=1)
    return x * cos + jnp.where(lo_mask, fwd, bwd) * sin


def _inproj_kernel(*refs, latent):
    if latent:
        (x_ref, mod_ref, gpre_ref, win_ref, gqk_ref, seg_ref, cosa_ref, sina_ref, cosb_ref, sinb_ref,
         qkv_ref, cd_ref) = refs
    else:
        (x_ref, mod_ref, gpre_ref, win_ref, gqk_ref, seg_ref,
         qkv_ref, cd_ref, oak_ref, oav_ref, obk_ref, obv_ref) = refs
    h = _rms(x_ref[...], gpre_ref[...]) * (1.0 + mod_ref[1:2, :]) + mod_ref[0:1, :]
    p = _dot(h.astype(BF16), win_ref[...])

    pa = p[:, COL_AQ:COL_AV]
    msq = _split_dot(pa * pa, seg_ref[...])
    pa = pa * lax.rsqrt(msq + RMS_EPS) * gqk_ref[...]
    pb = p[:, COL_BQ:COL_BV]
    if not latent:
        oak_ref[...] = pa[:, COL_AK:COL_AV]
        oav_ref[...] = p[:, COL_AV:COL_BQ]
        obk_ref[...] = p[:, COL_BK:COL_BV]
        obv_ref[...] = p[:, COL_BV:COL_DZ]
        a_cols = [pa[:, c * LANES:(c + 1) * LANES] for c in range(3)]
        b_cols = [pb[:, c * LANES:(c + 1) * LANES] for c in range(4)]
    else:
        lane = lax.broadcasted_iota(jnp.int32, (1, LANES), 1)
        lo_a = (lane % 32) < 16
        lo_b = (lane % 16) < 8
        cosa, sina, cosb, sinb = cosa_ref[...], sina_ref[...], cosb_ref[...], sinb_ref[...]
        a_cols = [_rope_cols(pa[:, c * LANES:(c + 1) * LANES], cosa, sina, 16, lo_a) for c in range(3)]
        b_cols = [_rope_cols(pb[:, c * LANES:(c + 1) * LANES], cosb, sinb, 8, lo_b) for c in range(4)]
    a_scale = A_HEAD_DIM ** -0.5
    b_scale = B_QK_DIM ** -0.5
    qkv_ref[:, 0:128] = (a_cols[0] * a_scale).astype(BF16)
    qkv_ref[:, 128:256] = (a_cols[1] * a_scale).astype(BF16)
    qkv_ref[:, 256:384] = a_cols[2].astype(BF16)
    qkv_ref[:, 384:512] = p[:, COL_AV:COL_BQ].astype(BF16)
    qkv_ref[:, 512:640] = (b_cols[0] * b_scale).astype(BF16)
    qkv_ref[:, 640:768] = (b_cols[1] * b_scale).astype(BF16)
    qkv_ref[:, 768:896] = b_cols[2].astype(BF16)
    qkv_ref[:, 896:1024] = b_cols[3].astype(BF16)
    qkv_ref[:, 1024:1280] = p[:, COL_BV:COL_DZ].astype(BF16)
    cd_ref[...] = p[:, COL_DZ:IN_WIDTH]


def _inproj(x, mods, layer, gpre, win, gqk, seg, rope, *, latent, seq):
    n = x.shape[0]
    tm = TOKEN_TILE
    tiles_per_seq = seq // tm if latent else 1
    if latent:
        mod_row = lambda i: 1 + i // tiles_per_seq
    else:
        mod_row = lambda i: 0
    in_specs = [
        pl.BlockSpec((tm, D_MODEL), lambda i: (i, 0)),
        pl.BlockSpec((None, None, N_MOD, D_MODEL), lambda i: (layer, mod_row(i), 0, 0)),
        pl.BlockSpec((None, 1, D_MODEL), lambda i: (layer, 0, 0)),
        pl.BlockSpec((None, D_MODEL, IN_WIDTH), lambda i: (layer, 0, 0)),
        pl.BlockSpec((None, 1, 384), lambda i: (layer, 0, 0)),
        pl.BlockSpec((384, 384), lambda i: (0, 0)),
    ]
    args = [x, mods, gpre, win, gqk, seg]
    out_specs = [pl.BlockSpec((tm, QKV_WIDTH), lambda i: (i, 0)),
                 pl.BlockSpec((tm, CD_WIDTH), lambda i: (i, 0))]
    out_shape = [jax.ShapeDtypeStruct((n, QKV_WIDTH), BF16), jax.ShapeDtypeStruct((n, CD_WIDTH), F32)]
    if latent:
        in_specs += [pl.BlockSpec((tm, LANES), lambda i: (i % tiles_per_seq, 0))] * 4
        args += list(rope)
    else:
        for w in (128, 128, 256, 256):
            out_specs.append(pl.BlockSpec((tm, w), lambda i: (i, 0)))
            out_shape.append(jax.ShapeDtypeStruct((n, w), F32))
    return pl.pallas_call(
        functools.partial(_inproj_kernel, latent=latent),
        grid=(n // tm,),
        in_specs=in_specs,
        out_specs=out_specs,
        out_shape=out_shape,
        compiler_params=pltpu.CompilerParams(
            dimension_semantics=("arbitrary",), vmem_limit_bytes=VMEM_LIMIT_BYTES),
        name="inproj_latent" if latent else "inproj_context",
    )(*args)


def _softmax_parts(qm, k_segs):
    ss = [_dot_nt(qm, k) for k in k_segs]
    m = functools.reduce(jnp.maximum, [jnp.max(s, axis=-1, keepdims=True) for s in ss])
    ps = [jnp.exp(s - m) for s in ss]
    l = functools.reduce(jnp.add, [jnp.sum(p, axis=-1, keepdims=True) for p in ps])
    return ps, l


def _pool_windows(xh, tq):
    n = tq + 2 * HALO
    up = lambda a, k: pltpu.roll(a, n - k, axis=0)
    a2 = xh + up(xh, 1)
    a4 = a2 + up(a2, 2)
    a8 = a4 + up(a4, 4)
    a16 = a8 + up(a8, 8)
    w2 = up(a2, HALO - 1)[0:tq]
    w4 = up(a4, HALO - 2)[0:tq]
    w8 = up(a8, HALO - 4)[0:tq]
    w16 = a16[0:tq]
    return w2, w4, w8, w16


def _mixer_kernel(*refs, latent, lam_init, tq, n_q):
    if latent:
        (aq_ref, ak_ref, av_ref, bq_ref, bk_ref, bv_ref, cak_ref, cav_ref, cbk_ref, cbv_ref,
         cx_ref, cxp_ref, cxn_ref, dz_ref, icnt_ref, lamp_ref, subln_ref, cw_ref, cscale_ref,
         gv_ref, ws_ref, bs_ref, out_ref) = refs
    else:
        (aq_ref, ak_ref, av_ref, bq_ref, bk_ref, bv_ref,
         cx_ref, cxp_ref, cxn_ref, dz_ref, icnt_ref, lamp_ref, subln_ref, cw_ref, cscale_ref,
         gv_ref, ws_ref, bs_ref, out_ref) = refs
    i = pl.program_id(1)
    lane = lax.broadcasted_iota(jnp.int32, (1, LANES), 1)
    half0 = lane < 64

    for c in range(2):
        qc = aq_ref[:, c * LANES:(c + 1) * LANES]
        k_segs = [ak_ref[...]]
        v_segs = [av_ref[...]]
        if latent:
            k_segs = [cak_ref[...].astype(BF16)] + k_segs
            v_segs = [cav_ref[...].astype(BF16)] + v_segs
        outs = []
        for u in range(2):
            qm = jnp.where(half0 if u == 0 else ~half0, qc, jnp.zeros_like(qc))
            ps, l = _softmax_parts(qm, k_segs)
            o = functools.reduce(jnp.add, [_dot(p.astype(BF16), v) for p, v in zip(ps, v_segs)])
            outs.append(o / l)
        out_ref[:, c * LANES:(c + 1) * LANES] = jnp.where(half0, outs[0], outs[1]).astype(BF16)

    lp = lamp_ref[...]
    lam = (jnp.exp(jnp.sum(lp[0:1] * lp[1:2], axis=-1, keepdims=True))
           - jnp.exp(jnp.sum(lp[2:3] * lp[3:4], axis=-1, keepdims=True)) + lam_init)
    for c in range(2):
        qc = bq_ref[:, c * LANES:(c + 1) * LANES]
        k_segs = [bk_ref[:, c * LANES:(c + 1) * LANES]]
        v_segs = [bv_ref[:, c * LANES:(c + 1) * LANES]]
        if latent:
            k_segs = [cbk_ref[:, c * LANES:(c + 1) * LANES].astype(BF16)] + k_segs
            v_segs = [cbv_ref[:, c * LANES:(c + 1) * LANES].astype(BF16)] + v_segs
        outs = []
        for u in range(2):
            parts = []
            for j in range(2):
                start = 64 * u + 32 * j
                sel = (lane >= start) & (lane < start + 32)
                parts.append(_softmax_parts(jnp.where(sel, qc, jnp.zeros_like(qc)), k_segs))
            (p0, l0), (p1, l1) = parts
            c0 = 1.0 / l0
            c1 = lam / l1
            o = functools.reduce(jnp.add, [
                _dot((a * c0 - b * c1).astype(BF16), v) for a, b, v in zip(p0, p1, v_segs)])
            valid = half0 if u == 0 else ~half0
            ms = jnp.sum(jnp.where(valid, o * o, 0.0), axis=-1, keepdims=True) * (1.0 / 64)
            outs.append(o * lax.rsqrt(ms + RMS_EPS))
        o = jnp.where(half0, outs[0], outs[1]) * subln_ref[...] * (1.0 - lam_init)
        out_ref[:, 256 + c * LANES:256 + (c + 1) * LANES] = o.astype(BF16)

    prev = jnp.where(i == 0, 0.0, cxp_ref[...])
    nxt = jnp.where(i == n_q - 1, 0.0, cxn_ref[...])
    cur = cx_ref[...]
    xh = jnp.concatenate([prev, cur, nxt], axis=0)
    w2, w4, _, _ = _pool_windows(xh[:, 0:LANES], tq)
    _, _, w8, w16 = _pool_windows(xh[:, LANES:2 * LANES], tq)
    sums = jnp.concatenate([jnp.where(half0, w2, w4), jnp.where(half0, w8, w16)], axis=1)
    pooled = sums * icnt_ref[...] - cur
    yc = _dot(pooled.astype(BF16), cw_ref[...]) * cscale_ref[...]
    out_ref[:, 512:768] = yc.astype(BF16)

    z = jax.nn.gelu(dz_ref[...])
    u_gate = z[:, 0:GROUP_W]
    v = _rms(z[:, GROUP_W:2 * GROUP_W], gv_ref[...])
    lane2 = lax.broadcasted_iota(jnp.int32, (1, GROUP_W), 1)
    for n in range(tq // CHUNK):
        vc = v[n * CHUNK:(n + 1) * CHUNK]
        stacked = jnp.concatenate(
            [jnp.where((lane2 >= 64 * g) & (lane2 < 64 * (g + 1)), vc, 0.0).astype(BF16) for g in range(4)],
            axis=0)
        sg = _dot(ws_ref[...], stacked) + bs_ref[...]
        out_ref[n * CHUNK:(n + 1) * CHUNK, 768:1024] = (u_gate[n * CHUNK:(n + 1) * CHUNK] * sg).astype(BF16)


def _mixers(qkv, cd, ctx, icnt, lamp, subln, cw, cscale, gv, ws, bs, layer, *, latent, seq, lam_init):
    n = qkv.shape[0]
    n_batch = n // seq
    tq = Q_TILE
    n_q = seq // tq
    row = lambda b, i: b * n_q + i
    row8 = tq // SUBLANES
    last8 = n // SUBLANES - 1
    in_specs = [
        pl.BlockSpec((tq, 256), lambda b, i: (row(b, i), 0)),
        pl.BlockSpec((seq, 128), lambda b, i: (b, 2)),
        pl.BlockSpec((seq, 128), lambda b, i: (b, 3)),
        pl.BlockSpec((tq, 256), lambda b, i: (row(b, i), 2)),
        pl.BlockSpec((seq, 256), lambda b, i: (b, 3)),
        pl.BlockSpec((seq, 256), lambda b, i: (b, 4)),
    ]
    args = [qkv] * 6
    if latent:
        for w in (128, 128, 256, 256):
            in_specs.append(pl.BlockSpec((None, None, 256, w), lambda b, i: (b, layer, 0, 0)))
        args += list(ctx)
    in_specs += [
        pl.BlockSpec((tq, 256), lambda b, i: (row(b, i), 2)),
        pl.BlockSpec((HALO, 256), lambda b, i: (jnp.maximum(row(b, i) * row8 - 1, 0), 2)),
        pl.BlockSpec((HALO, 256), lambda b, i: (jnp.minimum((row(b, i) + 1) * row8, last8), 2)),
        pl.BlockSpec((tq, 512), lambda b, i: (row(b, i), 0)),
        pl.BlockSpec((tq, 256), lambda b, i: (i, 0)),
        pl.BlockSpec((None, SUBLANES, LANES), lambda b, i: (layer, 0, 0)),
        pl.BlockSpec((None, 1, LANES), lambda b, i: (layer, 0, 0)),
        pl.BlockSpec((None, 256, 256), lambda b, i: (layer, 0, 0)),
        pl.BlockSpec((None, 1, 256), lambda b, i: (layer, 0, 0)),
        pl.BlockSpec((None, 1, 256), lambda b, i: (layer, 0, 0)),
        pl.BlockSpec((None, CHUNK, 4 * CHUNK), lambda b, i: (layer, 0, 0)),
        pl.BlockSpec((None, CHUNK, 256), lambda b, i: (layer, 0, 0)),
    ]
    args += [cd, cd, cd, cd, icnt, lamp, subln, cw, cscale, gv, ws, bs]
    return pl.pallas_call(
        functools.partial(_mixer_kernel, latent=latent, lam_init=lam_init, tq=tq, n_q=n_q),
        grid=(n_batch, n_q),
        in_specs=in_specs,
        out_specs=pl.BlockSpec((tq, D_MODEL), lambda b, i: (row(b, i), 0)),
        out_shape=jax.ShapeDtypeStruct((n, D_MODEL), BF16),
        compiler_params=pltpu.CompilerParams(
            dimension_semantics=("arbitrary", "arbitrary"), vmem_limit_bytes=VMEM_LIMIT_BYTES),
        name="mixers_latent" if latent else "mixers_context",
    )(*args)


def _ffn_kernel(mix_ref, x_ref, mod_ref, wout_ref, gpm_ref, gpf_ref, gpo_ref, wg_ref, wu_ref, wd_ref,
                out_ref, acc_ref):
    y = _dot(mix_ref[...], wout_ref[...])
    x1 = x_ref[...] + mod_ref[2:3, :] * _rms(y, gpm_ref[...])
    h = (_rms(x1, gpf_ref[...]) * (1.0 + mod_ref[4:5, :]) + mod_ref[3:4, :]).astype(BF16)
    acc_ref[...] = jnp.zeros_like(acc_ref)

    def body(c, carry):
        g = _dot(h, wg_ref[c])
        u = _dot(h, wu_ref[c])
        acc_ref[...] += _dot((jax.nn.silu(g) * u).astype(BF16), wd_ref[c])
        return carry

    lax.fori_loop(0, N_FF_CHUNKS, body, 0)
    out_ref[...] = x1 + mod_ref[5:6, :] * _rms(acc_ref[...], gpo_ref[...])


def _outproj_ffn(mixed, x, mods, layer, wout, gpm, gpf, gpo, wg, wu, wd, *, latent, seq):
    n = x.shape[0]
    tm = TOKEN_TILE
    tiles_per_seq = seq // tm if latent else 1
    if latent:
        mod_row = lambda i: 1 + i // tiles_per_seq
    else:
        mod_row = lambda i: 0
    const = dict(pipeline_mode=pl.Buffered(1))
    gain = lambda: pl.BlockSpec((None, 1, D_MODEL), lambda i: (layer, 0, 0))
    return pl.pallas_call(
        _ffn_kernel,
        grid=(n // tm,),
        in_specs=[
            pl.BlockSpec((tm, D_MODEL), lambda i: (i, 0)),
            pl.BlockSpec((tm, D_MODEL), lambda i: (i, 0)),
            pl.BlockSpec((None, None, N_MOD, D_MODEL), lambda i: (layer, mod_row(i), 0, 0)),
            pl.BlockSpec((None, D_MODEL, D_MODEL), lambda i: (layer, 0, 0), **const),
            gain(), gain(), gain(),
            pl.BlockSpec((None, N_FF_CHUNKS, D_MODEL, FF_CHUNK), lambda i: (layer, 0, 0, 0), **const),
            pl.BlockSpec((None, N_FF_CHUNKS, D_MODEL, FF_CHUNK), lambda i: (layer, 0, 0, 0), **const),
            pl.BlockSpec((None, N_FF_CHUNKS, FF_CHUNK, D_MODEL), lambda i: (layer, 0, 0, 0), **const),
        ],
        out_specs=pl.BlockSpec((tm, D_MODEL), lambda i: (i, 0)),
        out_shape=jax.ShapeDtypeStruct((n, D_MODEL), F32),
        scratch_shapes=[pltpu.VMEM((tm, D_MODEL), F32)],
        compiler_params=pltpu.CompilerParams(
            dimension_semantics=("arbitrary",), vmem_limit_bytes=VMEM_LIMIT_BYTES),
        name="outproj_ffn_latent" if latent else "outproj_ffn_context",
    )(mixed, x, mods, wout, gpm, gpf, gpo, wg, wu, wd)


def _rope_tables(seq):
    t = np.arange(seq)
    pos = {"row": (t // GRID_W).astype(np.float64), "col": (t % GRID_W).astype(np.float64)}

    def table(head_dim):
        half = head_dim // 2
        inv_freq = ROPE_THETA ** (-np.arange(0, half, 2, dtype=np.float64) / half)
        lane = np.arange(head_dim)
        within = lane % half
        freq = inv_freq[within % (half // 2)]
        p = np.where(lane[None, :] < half, pos["row"][:, None], pos["col"][:, None])
        ang = p * freq[None, :]
        sign = np.where(within < half // 2, -1.0, 1.0)
        reps = LANES // head_dim
        return (np.tile(np.cos(ang), (1, reps)).astype(np.float32),
                np.tile(np.sin(ang) * sign[None, :], (1, reps)).astype(np.float32))

    cosa, sina = table(A_HEAD_DIM)
    cosb, sinb = table(B_QK_DIM)
    return cosa, sina, cosb, sinb


def _inv_count_table(seq):
    t = np.arange(seq)
    cols = []
    for win in POOL_WINDOWS:
        cnt = np.minimum(t + win // 2, seq) - np.maximum(t - win // 2, 0)
        cols.append(np.repeat((1.0 / cnt)[:, None], 64, axis=1))
    return np.concatenate(cols, axis=1).astype(np.float32)


def _segment_mean_matrix():
    idx = np.arange(384) // A_HEAD_DIM
    return (idx[:, None] == idx[None, :]).astype(np.float32) / A_HEAD_DIM


def _in_column_order():
    head = lambda h: list(range(64 * h, 64 * (h + 1)))
    aq = head(0) + head(3) + head(1) + head(2)
    rest = list(range(256, 1280))
    cx = list(range(1280, 1536))
    dz = list(range(1536, 2048))
    return np.asarray(aq + rest + dz + cx), np.asarray(aq)


def kernel(x_prompt, x_sample, cache_a_k, cache_a_v, cache_b_k, cache_b_v, c, c_ctx, w_mod, b_mod, g_pre_mix, g_post_mix, g_pre_ffn, g_post_ffn, w_in, w_out, a_q_norm, a_k_norm, b_lq1, b_lk1, b_lq2, b_lk2, b_subln, c_w, c_scale, d_v_norm, d_ws, d_bs, w_gate, w_up, w_down):
    batch, seq, _ = x_prompt.shape
    dec_batch, dec_seq, _ = x_sample.shape
    past = cache_a_k.shape[2]

    in_cols, aq_rows = _in_column_order()
    win = w_in[:, :, in_cols].astype(BF16)
    wout = jnp.concatenate([w_out[:, aq_rows, :], w_out[:, 256:, :]], axis=1).astype(BF16)
    wg = w_gate.reshape(DEPTH, D_MODEL, N_FF_CHUNKS, FF_CHUNK).transpose(0, 2, 1, 3).astype(BF16)
    wu = w_up.reshape(DEPTH, D_MODEL, N_FF_CHUNKS, FF_CHUNK).transpose(0, 2, 1, 3).astype(BF16)
    wd = w_down.reshape(DEPTH, N_FF_CHUNKS, FF_CHUNK, D_MODEL).astype(BF16)
    gqk = jnp.concatenate([jnp.tile(a_q_norm, (1, 4)), jnp.tile(a_k_norm, (1, 2))], axis=1).reshape(DEPTH, 1, 384)
    seg = jnp.asarray(_segment_mean_matrix(), BF16)
    lamp = jnp.stack([b_lq1, b_lk1, b_lq2, b_lk2], axis=1)
    lamp = jnp.pad(lamp, ((0, 0), (0, SUBLANES - 4), (0, LANES - B_QK_DIM)))
    subln = jnp.tile(b_subln, (1, 2)).reshape(DEPTH, 1, LANES)
    eye = jnp.eye(4, dtype=F32)
    cw = (eye[None, :, None, :, None] * c_w[:, :, :, None, :]).reshape(DEPTH, 256, 256).astype(BF16)
    cscale = c_scale.reshape(DEPTH, 1, GROUP_W)
    gv = d_v_norm.reshape(DEPTH, 1, GROUP_W)
    ws = d_ws.transpose(0, 2, 1, 3).reshape(DEPTH, CHUNK, 4 * CHUNK).astype(BF16)
    bs = jnp.repeat(d_bs.transpose(0, 2, 1), 64, axis=2)
    gpre = g_pre_mix.reshape(DEPTH, 1, D_MODEL)
    gpm = g_post_mix.reshape(DEPTH, 1, D_MODEL)
    gpf = g_pre_ffn.reshape(DEPTH, 1, D_MODEL)
    gpo = g_post_ffn.reshape(DEPTH, 1, D_MODEL)
    rope = tuple(jnp.asarray(t) for t in _rope_tables(dec_seq))
    icnt_ctx = jnp.asarray(_inv_count_table(seq))
    icnt_lat = jnp.asarray(_inv_count_table(dec_seq))
    ctx = (cache_a_k.reshape(dec_batch, DEPTH, past, 128), cache_a_v.reshape(dec_batch, DEPTH, past, 128),
           cache_b_k.reshape(dec_batch, DEPTH, past, 256), cache_b_v.reshape(dec_batch, DEPTH, past, 256))

    cond8 = jnp.concatenate([c_ctx[None, :], c, jnp.zeros((SUBLANES - 1 - dec_batch, D_MODEL), F32)], axis=0)
    mods = _modulation(cond8, w_mod, b_mod)

    yp = x_prompt.reshape(batch * seq, D_MODEL)
    ys = x_sample.reshape(dec_batch * dec_seq, D_MODEL)
    own = [[], [], [], []]
    for layer in range(DEPTH):
        lam_init = 0.8 - 0.6 * math.exp(-0.3 * layer)
        mix_args = (lamp, subln, cw, cscale, gv, ws, bs, layer)
        ffn_args = (mods, layer, wout, gpm, gpf, gpo, wg, wu, wd)
        qkv, cd, oak, oav, obk, obv = _inproj(yp, mods, layer, gpre, win, gqk, seg, None, latent=False, seq=seq)
        for dst, val in zip(own, (oak, oav, obk, obv)):
            dst.append(val)
        mixed = _mixers(qkv, cd, None, icnt_ctx, *mix_args, latent=False, seq=seq, lam_init=lam_init)
        yp = _outproj_ffn(mixed, yp, *ffn_args, latent=False, seq=seq)
        qkv, cd = _inproj(ys, mods, layer, gpre, win, gqk, seg, rope, latent=True, seq=dec_seq)
        mixed = _mixers(qkv, cd, ctx, icnt_lat, *mix_args, latent=True, seq=dec_seq, lam_init=lam_init)
        ys = _outproj_ffn(mixed, ys, *ffn_args, latent=True, seq=dec_seq)

    new_a_k = jnp.stack(own[0], axis=1).reshape(batch, seq, DEPTH, 2, 64).transpose(0, 2, 1, 3, 4)
    new_a_v = jnp.stack(own[1], axis=1).reshape(batch, seq, DEPTH, 2, 64).transpose(0, 2, 1, 3, 4)
    new_b_k = jnp.stack(own[2], axis=1).reshape(batch, seq, DEPTH, 4, 2, 32).transpose(0, 2, 1, 3, 4, 5)
    new_b_v = jnp.stack(own[3], axis=1).reshape(batch, seq, DEPTH, 4, 64).transpose(0, 2, 1, 3, 4)
    return (yp.reshape(batch, seq, D_MODEL), ys.reshape(dec_batch, dec_seq, D_MODEL),
            new_a_k, new_a_v, new_b_k, new_b_v)
```

```python
import functools
import math

import numpy as np
import jax
import jax.numpy as jnp
from jax import lax
from jax.experimental import pallas as pl
from jax.experimental.pallas import tpu as pltpu

F32 = jnp.float32
BF16 = jnp.bfloat16

D_MODEL = 1024
DEPTH = 4
GRID_W = 64
GROUP_W = 256
A_HEAD_DIM = 64
B_QK_DIM = 32
POOL_WINDOWS = (2, 4, 8, 16)
CHUNK = 128
D_FF = 2816
N_MOD = 6
RMS_EPS = 1e-6
LOG2E = math.log2(math.e)
ROPE_THETA = 10000.0
IN_WIDTH = 2048

LANES = 128
SUBLANES = 8
VMEM_LIMIT_BYTES = 56 * 1024 * 1024

COL_AQ, COL_AK, COL_AV, COL_BQ, COL_BK, COL_BV, COL_CX, COL_DZ = 0, 256, 384, 512, 768, 1024, 1280, 1536
QKV_WIDTH = 1280
CD_WIDTH = 768
FF_CHUNK = 256
N_FF_CHUNKS = D_FF // FF_CHUNK

TOKEN_TILE = 512
Q_TILE = 256
MOD_COL_TILE = 1536
HALO = 8


def _rms(x, g):
    ms = jnp.mean(x * x, axis=-1, keepdims=True)
    return x * lax.rsqrt(ms + RMS_EPS) * g


def _dot(a, b):
    return jnp.dot(a, b, preferred_element_type=F32)


def _dot_nt(a, b):
    return lax.dot_general(a, b, (((1,), (1,)), ((), ())), preferred_element_type=F32)


def _pair_heads(c0, c1, half0):
    return jnp.where(half0, c0, c1), pltpu.roll(jnp.where(half0, c1, c0), 64, axis=1)


def _unpair_heads(d0, d1, half0):
    r = pltpu.roll(d1, 64, axis=1)
    return jnp.where(half0, d0, r), jnp.where(half0, r, d0)


def _split_dot(x, w):
    hi = x.astype(BF16)
    lo = (x - hi.astype(F32)).astype(BF16)
    return _dot(hi, w) + _dot(lo, w)


def _mod_kernel(cond_ref, w_ref, b_ref, out_ref):
    s = jax.nn.silu(cond_ref[...]).astype(BF16)
    out_ref[...] = _dot(s, w_ref[...].astype(BF16)) + b_ref[...]


def _modulation(cond8, w_mod, b_mod):
    n_col = (N_MOD * D_MODEL) // MOD_COL_TILE
    out = pl.pallas_call(
        _mod_kernel,
        grid=(DEPTH, n_col),
        in_specs=[
            pl.BlockSpec((SUBLANES, D_MODEL), lambda l, j: (0, 0)),
            pl.BlockSpec((None, D_MODEL, MOD_COL_TILE), lambda l, j: (l, 0, j)),
            pl.BlockSpec((None, 1, MOD_COL_TILE), lambda l, j: (l, 0, j)),
        ],
        out_specs=pl.BlockSpec((None, SUBLANES, MOD_COL_TILE), lambda l, j: (l, 0, j)),
        out_shape=jax.ShapeDtypeStruct((DEPTH, SUBLANES, N_MOD * D_MODEL), F32),
        compiler_params=pltpu.CompilerParams(
            dimension_semantics=("arbitrary", "arbitrary"), vmem_limit_bytes=VMEM_LIMIT_BYTES),
        name="modulation",
    )(cond8, w_mod, b_mod.reshape(DEPTH, 1, N_MOD * D_MODEL))
    return out.reshape(DEPTH, SUBLANES, N_MOD, D_MODEL)


def _rope_cols(x, cos, sin, shift, lo_mask):
    fwd = pltpu.roll(x, LANES - shift, axis=1)
    bwd = pltpu.roll(x, shift, axis=1)
    return x * cos + jnp.where(lo_mask, fwd, bwd) * sin


def _inproj_kernel(*refs, latent):
    if latent:
        (x_ref, mod_ref, gpre_ref, win_ref, gqk_ref, seg_ref, cosa_ref, sina_ref, cosb_ref, sinb_ref,
         qkv_ref, cd_ref) = refs
    else:
        (x_ref, mod_ref, gpre_ref, win_ref, gqk_ref, seg_ref,
         qkv_ref, cd_ref, oak_ref, oav_ref, obk_ref, obv_ref) = refs
    h = _rms(x_ref[...], gpre_ref[...]) * (1.0 + mod_ref[1:2, :]) + mod_ref[0:1, :]
    p = _dot(h.astype(BF16), win_ref[...])

    pa = p[:, COL_AQ:COL_AV]
    msq = _split_dot(pa * pa, seg_ref[...])
    pa = pa * lax.rsqrt(msq + RMS_EPS) * gqk_ref[...]
    pb = p[:, COL_BQ:COL_BV]
    lane = lax.broadcasted_iota(jnp.int32, (1, LANES), 1)
    if not latent:
        oak_ref[...] = pa[:, COL_AK:COL_AV]
        oav_ref[...] = p[:, COL_AV:COL_BQ]
        obk_ref[...] = p[:, COL_BK:COL_BV]
        obv_ref[...] = p[:, COL_BV:COL_CX]
        a_cols = [pa[:, c * LANES:(c + 1) * LANES] for c in range(3)]
        b_cols = [pb[:, c * LANES:(c + 1) * LANES] for c in range(4)]
    else:
        lo_a = (lane % 32) < 16
        lo_b = (lane % 16) < 8
        cosa, sina, cosb, sinb = cosa_ref[...], sina_ref[...], cosb_ref[...], sinb_ref[...]
        a_cols = [_rope_cols(pa[:, c * LANES:(c + 1) * LANES], cosa, sina, 16, lo_a) for c in range(3)]
        b_cols = [_rope_cols(pb[:, c * LANES:(c + 1) * LANES], cosb, sinb, 8, lo_b) for c in range(4)]
    a_scale = A_HEAD_DIM ** -0.5 * LOG2E
    b_scale = B_QK_DIM ** -0.5 * LOG2E
    aq0, aq1 = _pair_heads(a_cols[0] * a_scale, a_cols[1] * a_scale, lane < 64)
    qkv_ref[:, 0:128] = aq0.astype(BF16)
    qkv_ref[:, 128:256] = aq1.astype(BF16)
    qkv_ref[:, 256:384] = a_cols[2].astype(BF16)
    qkv_ref[:, 384:512] = p[:, COL_AV:COL_BQ].astype(BF16)
    qkv_ref[:, 512:640] = (b_cols[0] * b_scale).astype(BF16)
    qkv_ref[:, 640:768] = (b_cols[1] * b_scale).astype(BF16)
    qkv_ref[:, 768:896] = b_cols[2].astype(BF16)
    qkv_ref[:, 896:1024] = b_cols[3].astype(BF16)
    qkv_ref[:, 1024:1280] = p[:, COL_BV:COL_CX].astype(BF16)
    cd_ref[:, 0:512] = p[:, COL_DZ:IN_WIDTH]
    cd_ref[:, 512:768] = p[:, COL_CX:COL_DZ]


def _inproj(x, mods, layer, gpre, win, gqk, seg, rope, *, latent, seq):
    n = x.shape[0]
    tm = TOKEN_TILE
    tiles_per_seq = seq // tm if latent else 1
    if latent:
        mod_row = lambda i: 1 + i // tiles_per_seq
    else:
        mod_row = lambda i: 0
    in_specs = [
        pl.BlockSpec((tm, D_MODEL), lambda i: (i, 0)),
        pl.BlockSpec((None, None, N_MOD, D_MODEL), lambda i: (layer, mod_row(i), 0, 0)),
        pl.BlockSpec((None, 1, D_MODEL), lambda i: (layer, 0, 0)),
        pl.BlockSpec((None, D_MODEL, IN_WIDTH), lambda i: (layer, 0, 0)),
        pl.BlockSpec((None, 1, 384), lambda i: (layer, 0, 0)),
        pl.BlockSpec((384, 384), lambda i: (0, 0)),
    ]
    args = [x, mods, gpre, win, gqk, seg]
    out_specs = [pl.BlockSpec((tm, QKV_WIDTH), lambda i: (i, 0)),
                 pl.BlockSpec((tm, CD_WIDTH), lambda i: (i, 0))]
    out_shape = [jax.ShapeDtypeStruct((n, QKV_WIDTH), BF16), jax.ShapeDtypeStruct((n, CD_WIDTH), F32)]
    if latent:
        in_specs += [pl.BlockSpec((tm, LANES), lambda i: (i % tiles_per_seq, 0))] * 4
        args += list(rope)
    else:
        for w in (128, 128, 256, 256):
            out_specs.append(pl.BlockSpec((tm, w), lambda i: (i, 0)))
            out_shape.append(jax.ShapeDtypeStruct((n, w), F32))
    return pl.pallas_call(
        functools.partial(_inproj_kernel, latent=latent),
        grid=(n // tm,),
        in_specs=in_specs,
        out_specs=out_specs,
        out_shape=out_shape,
        compiler_params=pltpu.CompilerParams(
            dimension_semantics=("arbitrary",), vmem_limit_bytes=VMEM_LIMIT_BYTES),
        name="inproj_latent" if latent else "inproj_context",
    )(*args)


def _softmax_parts(qm, k_segs):
    ss = [_dot_nt(qm, k) for k in k_segs]
    m = functools.reduce(jnp.maximum, [jnp.max(s, axis=-1, keepdims=True) for s in ss])
    ps = [jnp.exp2(s - m) for s in ss]
    l = functools.reduce(jnp.add, [jnp.sum(p, axis=-1, keepdims=True) for p in ps])
    return ps, l


def _pool_windows(xh, tq):
    n = tq + 2 * HALO
    up = lambda a, k: pltpu.roll(a, n - k, axis=0)
    a2 = xh + up(xh, 1)
    a4 = a2 + up(a2, 2)
    a8 = a4 + up(a4, 4)
    a16 = a8 + up(a8, 8)
    w2 = up(a2, HALO - 1)[0:tq]
    w4 = up(a4, HALO - 2)[0:tq]
    w8 = up(a8, HALO - 4)[0:tq]
    w16 = a16[0:tq]
    return w2, w4, w8, w16


def _mixer_kernel(*refs, latent, lam_init, tq, n_q):
    if latent:
        (aq_ref, ak_ref, av_ref, bq_ref, bk_ref, bv_ref, cak_ref, cav_ref, cbk_ref, cbv_ref,
         cx_ref, cxp_ref, cxn_ref, dz_ref, icnt_ref, lamp_ref, subln_ref, cw_ref, cscale_ref,
         gv_ref, ws_ref, bs_ref, out_ref) = refs
    else:
        (aq_ref, ak_ref, av_ref, bq_ref, bk_ref, bv_ref,
         cx_ref, cxp_ref, cxn_ref, dz_ref, icnt_ref, lamp_ref, subln_ref, cw_ref, cscale_ref,
         gv_ref, ws_ref, bs_ref, out_ref) = refs
    i = pl.program_id(1)
    lane = lax.broadcasted_iota(jnp.int32, (1, LANES), 1)
    half0 = lane < 64

    a_cols = []
    for c in range(2):
        qc = aq_ref[:, c * LANES:(c + 1) * LANES]
        k_segs = [ak_ref[...]]
        v_segs = [av_ref[...]]
        if latent:
            k_segs = [cak_ref[...].astype(BF16)] + k_segs
            v_segs = [cav_ref[...].astype(BF16)] + v_segs
        outs = []
        for u in range(2):
            qm = jnp.where(half0 if u == 0 else ~half0, qc, jnp.zeros_like(qc))
            ps, l = _softmax_parts(qm, k_segs)
            o = functools.reduce(jnp.add, [_dot(p.astype(BF16), v) for p, v in zip(ps, v_segs)])
            outs.append(o / l)
        a_cols.append(jnp.where(half0, outs[0], outs[1]))
    a0, a1 = _unpair_heads(a_cols[0], a_cols[1], half0)
    out_ref[:, 0:128] = a0.astype(BF16)
    out_ref[:, 128:256] = a1.astype(BF16)

    lp = lamp_ref[...]
    lam = (jnp.exp(jnp.sum(lp[0:1] * lp[1:2], axis=-1, keepdims=True))
           - jnp.exp(jnp.sum(lp[2:3] * lp[3:4], axis=-1, keepdims=True)) + lam_init)
    for c in range(2):
        qc = bq_ref[:, c * LANES:(c + 1) * LANES]
        k_segs = [bk_ref[:, c * LANES:(c + 1) * LANES]]
        v_segs = [bv_ref[:, c * LANES:(c + 1) * LANES]]
        if latent:
            k_segs = [cbk_ref[:, c * LANES:(c + 1) * LANES].astype(BF16)] + k_segs
            v_segs = [cbv_ref[:, c * LANES:(c + 1) * LANES].astype(BF16)] + v_segs
        outs = []
        for u in range(2):
            parts = []
            for j in range(2):
                start = 64 * u + 32 * j
                sel = (lane >= start) & (lane < start + 32)
                parts.append(_softmax_parts(jnp.where(sel, qc, jnp.zeros_like(qc)), k_segs))
            (p0, l0), (p1, l1) = parts
            c0 = 1.0 / l0
            c1 = lam / l1
            o = functools.reduce(jnp.add, [
                _dot((a * c0 - b * c1).astype(BF16), v) for a, b, v in zip(p0, p1, v_segs)])
            valid = half0 if u == 0 else ~half0
            ms = jnp.sum(jnp.where(valid, o * o, 0.0), axis=-1, keepdims=True) * (1.0 / 64)
            outs.append(o * lax.rsqrt(ms + RMS_EPS))
        o = jnp.where(half0, outs[0], outs[1]) * subln_ref[...] * (1.0 - lam_init)
        out_ref[:, 256 + c * LANES:256 + (c + 1) * LANES] = o.astype(BF16)

    prev = jnp.where(i == 0, 0.0, cxp_ref[...])
    nxt = jnp.where(i == n_q - 1, 0.0, cxn_ref[...])
    cur = cx_ref[...]
    xh = jnp.concatenate([prev, cur, nxt], axis=0)
    w2, w4, _, _ = _pool_windows(xh[:, 0:LANES], tq)
    _, _, w8, w16 = _pool_windows(xh[:, LANES:2 * LANES], tq)
    sums = jnp.concatenate([jnp.where(half0, w2, w4), jnp.where(half0, w8, w16)], axis=1)
    pooled = sums * icnt_ref[...] - cur
    yc = _dot(pooled.astype(BF16), cw_ref[...]) * cscale_ref[...]
    out_ref[:, 512:768] = yc.astype(BF16)

    z = jax.nn.gelu(dz_ref[...])
    u_gate = z[:, 0:GROUP_W]
    v = _rms(z[:, GROUP_W:2 * GROUP_W], gv_ref[...])
    lane2 = lax.broadcasted_iota(jnp.int32, (1, GROUP_W), 1)
    for n in range(tq // CHUNK):
        vc = v[n * CHUNK:(n + 1) * CHUNK]
        stacked = jnp.concatenate(
            [jnp.where((lane2 >= 64 * g) & (lane2 < 64 * (g + 1)), vc, 0.0).astype(BF16) for g in range(4)],
            axis=0)
        sg = _dot(ws_ref[...], stacked) + bs_ref[...]
        out_ref[n * CHUNK:(n + 1) * CHUNK, 768:1024] = (u_gate[n * CHUNK:(n + 1) * CHUNK] * sg).astype(BF16)


def _mixers(qkv, cd, ctx, icnt, lamp, subln, cw, cscale, gv, ws, bs, layer, *, latent, seq, lam_init):
    n = qkv.shape[0]
    n_batch = n // seq
    tq = Q_TILE
    n_q = seq // tq
    row = lambda b, i: b * n_q + i
    row8 = tq // SUBLANES
    last8 = n // SUBLANES - 1
    in_specs = [
        pl.BlockSpec((tq, 256), lambda b, i: (row(b, i), 0)),
        pl.BlockSpec((seq, 128), lambda b, i: (b, 2)),
        pl.BlockSpec((seq, 128), lambda b, i: (b, 3)),
        pl.BlockSpec((tq, 256), lambda b, i: (row(b, i), 2)),
        pl.BlockSpec((seq, 256), lambda b, i: (b, 3)),
        pl.BlockSpec((seq, 256), lambda b, i: (b, 4)),
    ]
    args = [qkv] * 6
    if latent:
        for w in (128, 128, 256, 256):
            in_specs.append(pl.BlockSpec((None, None, 256, w), lambda b, i: (b, layer, 0, 0)))
        args += list(ctx)
    in_specs += [
        pl.BlockSpec((tq, 256), lambda b, i: (row(b, i), 2)),
        pl.BlockSpec((HALO, 256), lambda b, i: (jnp.maximum(row(b, i) * row8 - 1, 0), 2)),
        pl.BlockSpec((HALO, 256), lambda b, i: (jnp.minimum((row(b, i) + 1) * row8, last8), 2)),
        pl.BlockSpec((tq, 512), lambda b, i: (row(b, i), 0)),
        pl.BlockSpec((tq, 256), lambda b, i: (i, 0)),
        pl.BlockSpec((None, SUBLANES, LANES), lambda b, i: (layer, 0, 0)),
        pl.BlockSpec((None, 1, LANES), lambda b, i: (layer, 0, 0)),
        pl.BlockSpec((None, 256, 256), lambda b, i: (layer, 0, 0)),
        pl.BlockSpec((None, 1, 256), lambda b, i: (layer, 0, 0)),
        pl.BlockSpec((None, 1, 256), lambda b, i: (layer, 0, 0)),
        pl.BlockSpec((None, CHUNK, 4 * CHUNK), lambda b, i: (layer, 0, 0)),
        pl.BlockSpec((None, CHUNK, 256), lambda b, i: (layer, 0, 0)),
    ]
    args += [cd, cd, cd, cd, icnt, lamp, subln, cw, cscale, gv, ws, bs]
    return pl.pallas_call(
        functools.partial(_mixer_kernel, latent=latent, lam_init=lam_init, tq=tq, n_q=n_q),
        grid=(n_batch, n_q),
        in_specs=in_specs,
        out_specs=pl.BlockSpec((tq, D_MODEL), lambda b, i: (row(b, i), 0)),
        out_shape=jax.ShapeDtypeStruct((n, D_MODEL), BF16),
        compiler_params=pltpu.CompilerParams(
            dimension_semantics=("arbitrary", "arbitrary"), vmem_limit_bytes=VMEM_LIMIT_BYTES),
        name="mixers_latent" if latent else "mixers_context",
    )(*args)


def _ffn_kernel(mix_ref, x_ref, mod_ref, wout_ref, gpm_ref, gpf_ref, gpo_ref, wg_ref, wu_ref, wd_ref,
                out_ref, acc_ref):
    y = _dot(mix_ref[...], wout_ref[...])
    x1 = x_ref[...] + mod_ref[2:3, :] * _rms(y, gpm_ref[...])
    h = (_rms(x1, gpf_ref[...]) * (1.0 + mod_ref[4:5, :]) + mod_ref[3:4, :]).astype(BF16)
    for c in range(N_FF_CHUNKS):
        cols = slice(c * FF_CHUNK, (c + 1) * FF_CHUNK)
        g = _dot(h, wg_ref[:, cols])
        u = _dot(h, wu_ref[:, cols])
        part = _dot((jax.nn.silu(g) * u).astype(BF16), wd_ref[cols, :])
        if c == 0:
            acc_ref[...] = part
        else:
            acc_ref[...] += part
    out_ref[...] = x1 + mod_ref[5:6, :] * _rms(acc_ref[...], gpo_ref[...])


def _outproj_ffn(mixed, x, mods, layer, wout, gpm, gpf, gpo, wg, wu, wd, *, latent, seq):
    n = x.shape[0]
    tm = TOKEN_TILE
    tiles_per_seq = seq // tm if latent else 1
    if latent:
        mod_row = lambda i: 1 + i // tiles_per_seq
    else:
        mod_row = lambda i: 0
    const = dict(pipeline_mode=pl.Buffered(1))
    gain = lambda: pl.BlockSpec((None, 1, D_MODEL), lambda i: (layer, 0, 0))
    return pl.pallas_call(
        _ffn_kernel,
        grid=(n // tm,),
        in_specs=[
            pl.BlockSpec((tm, D_MODEL), lambda i: (i, 0)),
            pl.BlockSpec((tm, D_MODEL), lambda i: (i, 0)),
            pl.BlockSpec((None, None, N_MOD, D_MODEL), lambda i: (layer, mod_row(i), 0, 0)),
            pl.BlockSpec((None, D_MODEL, D_MODEL), lambda i: (layer, 0, 0), **const),
            gain(), gain(), gain(),
            pl.BlockSpec((None, D_MODEL, D_FF), lambda i: (layer, 0, 0), **const),
            pl.BlockSpec((None, D_MODEL, D_FF), lambda i: (layer, 0, 0), **const),
            pl.BlockSpec((None, D_FF, D_MODEL), lambda i: (layer, 0, 0), **const),
        ],
        out_specs=pl.BlockSpec((tm, D_MODEL), lambda i: (i, 0)),
        out_shape=jax.ShapeDtypeStruct((n, D_MODEL), F32),
        scratch_shapes=[pltpu.VMEM((tm, D_MODEL), F32)],
        compiler_params=pltpu.CompilerParams(
            dimension_semantics=("arbitrary",), vmem_limit_bytes=VMEM_LIMIT_BYTES),
        name="outproj_ffn_latent" if latent else "outproj_ffn_context",
    )(mixed, x, mods, wout, gpm, gpf, gpo, wg, wu, wd)


def _rope_tables(seq):
    t = np.arange(seq)
    pos = {"row": (t // GRID_W).astype(np.float64), "col": (t % GRID_W).astype(np.float64)}

    def table(head_dim):
        half = head_dim // 2
        inv_freq = ROPE_THETA ** (-np.arange(0, half, 2, dtype=np.float64) / half)
        lane = np.arange(head_dim)
        within = lane % half
        freq = inv_freq[within % (half // 2)]
        p = np.where(lane[None, :] < half, pos["row"][:, None], pos["col"][:, None])
        ang = p * freq[None, :]
        sign = np.where(within < half // 2, -1.0, 1.0)
        reps = LANES // head_dim
        return (np.tile(np.cos(ang), (1, reps)).astype(np.float32),
                np.tile(np.sin(ang) * sign[None, :], (1, reps)).astype(np.float32))

    cosa, sina = table(A_HEAD_DIM)
    cosb, sinb = table(B_QK_DIM)
    return cosa, sina, cosb, sinb


def _inv_count_table(seq):
    t = np.arange(seq)
    cols = []
    for win in POOL_WINDOWS:
        cnt = np.minimum(t + win // 2, seq) - np.maximum(t - win // 2, 0)
        cols.append(np.repeat((1.0 / cnt)[:, None], 64, axis=1))
    return np.concatenate(cols, axis=1).astype(np.float32)


def _segment_mean_matrix():
    idx = np.arange(384) // A_HEAD_DIM
    return (idx[:, None] == idx[None, :]).astype(np.float32) / A_HEAD_DIM


def kernel(x_prompt, x_sample, cache_a_k, cache_a_v, cache_b_k, cache_b_v, c, c_ctx, w_mod, b_mod, g_pre_mix, g_post_mix, g_pre_ffn, g_post_ffn, w_in, w_out, a_q_norm, a_k_norm, b_lq1, b_lk1, b_lq2, b_lk2, b_subln, c_w, c_scale, d_v_norm, d_ws, d_bs, w_gate, w_up, w_down):
    batch, seq, _ = x_prompt.shape
    dec_batch, dec_seq, _ = x_sample.shape
    past = cache_a_k.shape[2]

    win = w_in.astype(BF16)
    wout = w_out.astype(BF16)
    wg = w_gate.astype(BF16)
    wu = w_up.astype(BF16)
    wd = w_down.astype(BF16)
    gqk = jnp.concatenate([jnp.tile(a_q_norm, (1, 4)), jnp.tile(a_k_norm, (1, 2))], axis=1).reshape(DEPTH, 1, 384)
    seg = jnp.asarray(_segment_mean_matrix(), BF16)
    lamp = jnp.stack([b_lq1, b_lk1, b_lq2, b_lk2], axis=1)
    lamp = jnp.pad(lamp, ((0, 0), (0, SUBLANES - 4), (0, LANES - B_QK_DIM)))
    subln = jnp.tile(b_subln, (1, 2)).reshape(DEPTH, 1, LANES)
    eye = jnp.eye(4, dtype=F32)
    cw = (eye[None, :, None, :, None] * c_w[:, :, :, None, :]).reshape(DEPTH, 256, 256).astype(BF16)
    cscale = c_scale.reshape(DEPTH, 1, GROUP_W)
    gv = d_v_norm.reshape(DEPTH, 1, GROUP_W)
    ws = d_ws.transpose(0, 2, 1, 3).reshape(DEPTH, CHUNK, 4 * CHUNK).astype(BF16)
    bs = jnp.repeat(d_bs.transpose(0, 2, 1), 64, axis=2)
    gpre = g_pre_mix.reshape(DEPTH, 1, D_MODEL)
    gpm = g_post_mix.reshape(DEPTH, 1, D_MODEL)
    gpf = g_pre_ffn.reshape(DEPTH, 1, D_MODEL)
    gpo = g_post_ffn.reshape(DEPTH, 1, D_MODEL)
    rope = tuple(jnp.asarray(t) for t in _rope_tables(dec_seq))
    icnt_ctx = jnp.asarray(_inv_count_table(seq))
    icnt_lat = jnp.asarray(_inv_count_table(dec_seq))
    ctx = (cache_a_k.reshape(dec_batch, DEPTH, past, 128), cache_a_v.reshape(dec_batch, DEPTH, past, 128),
           cache_b_k.reshape(dec_batch, DEPTH, past, 256), cache_b_v.reshape(dec_batch, DEPTH, past, 256))

    cond8 = jnp.concatenate([c_ctx[None, :], c, jnp.zeros((SUBLANES - 1 - dec_batch, D_MODEL), F32)], axis=0)
    mods = _modulation(cond8, w_mod, b_mod)

    yp = x_prompt.reshape(batch * seq, D_MODEL)
    ys = x_sample.reshape(dec_batch * dec_seq, D_MODEL)
    own = [[], [], [], []]
    for layer in range(DEPTH):
        lam_init = 0.8 - 0.6 * math.exp(-0.3 * layer)
        mix_args = (lamp, subln, cw, cscale, gv, ws, bs, layer)
        ffn_args = (mods, layer, wout, gpm, gpf, gpo, wg, wu, wd)
        qkv, cd, oak, oav, obk, obv = _inproj(yp, mods, layer, gpre, win, gqk, seg, None, latent=False, seq=seq)
        for dst, val in zip(own, (oak, oav, obk, obv)):
            dst.append(val)
        mixed = _mixers(qkv, cd, None, icnt_ctx, *mix_args, latent=False, seq=seq, lam_init=lam_init)
        yp = _outproj_ffn(mixed, yp, *ffn_args, latent=False, seq=seq)
        qkv, cd = _inproj(ys, mods, layer, gpre, win, gqk, seg, rope, latent=True, seq=dec_seq)
        mixed = _mixers(qkv, cd, ctx, icnt_lat, *mix_args, latent=True, seq=dec_seq, lam_init=lam_init)
        ys = _outproj_ffn(mixed, ys, *ffn_args, latent=True, seq=dec_seq)

    stack = lambda vals, tail: jnp.stack([v.reshape((batch, seq) + tail) for v in vals], axis=1)
    new_a_k = stack(own[0], (2, 64))
    new_a_v = stack(own[1], (2, 64))
    new_b_k = stack(own[2], (4, 2, 32))
    new_b_v = stack(own[3], (4, 64))
    return (yp.reshape(batch, seq, D_MODEL), ys.reshape(dec_batch, dec_seq, D_MODEL),
            new_a_k, new_a_v, new_b_k, new_b_v)
```

```python
import functools
import math

import numpy as np
import jax
import jax.numpy as jnp
from jax import lax
from jax.experimental import pallas as pl
from jax.experimental.pallas import tpu as pltpu

F32 = jnp.float32
BF16 = jnp.bfloat16

D_MODEL = 1024
DEPTH = 4
GRID_W = 64
GROUP_W = 256
A_HEAD_DIM = 64
B_QK_DIM = 32
POOL_WINDOWS = (2, 4, 8, 16)
CHUNK = 128
D_FF = 2816
N_MOD = 6
RMS_EPS = 1e-6
LOG2E = math.log2(math.e)
ROPE_THETA = 10000.0
IN_WIDTH = 2048

LANES = 128
SUBLANES = 8
VMEM_LIMIT_BYTES = 56 * 1024 * 1024

COL_AQ, COL_AK, COL_AV, COL_BQ, COL_BK, COL_BV, COL_CX, COL_DZ = 0, 256, 384, 512, 768, 1024, 1280, 1536
QKV_AQ, QKV_AV, QKV_BQ, QKV_BK, QKV_BV, QKV_AK = 0, 256, 512, 768, 1024, 1280
QKV_WIDTH = 1408
CD_WIDTH = 768
FF_CHUNK = 256
N_FF_CHUNKS = D_FF // FF_CHUNK

TOKEN_TILE = 512
Q_TILE = 256
MOD_COL_TILE = 1536
HALO = 8


def _rms(x, g):
    ms = jnp.mean(x * x, axis=-1, keepdims=True)
    return x * lax.rsqrt(ms + RMS_EPS) * g


def _dot(a, b):
    return jnp.dot(a, b, preferred_element_type=F32)


def _dot_nt(a, b):
    return lax.dot_general(a, b, (((1,), (1,)), ((), ())), preferred_element_type=F32)


def _pair_heads(c0, c1, half0):
    return jnp.where(half0, c0, c1), pltpu.roll(jnp.where(half0, c1, c0), 64, axis=1)


def _unpair_heads(d0, d1, half0):
    r = pltpu.roll(d1, 64, axis=1)
    return jnp.where(half0, d0, r), jnp.where(half0, r, d0)


def _split_dot(x, w):
    hi = x.astype(BF16)
    lo = (x - hi.astype(F32)).astype(BF16)
    return _dot(hi, w) + _dot(lo, w)


def _mod_kernel(cond_ref, w_ref, b_ref, out_ref):
    s = jax.nn.silu(cond_ref[...]).astype(BF16)
    out_ref[...] = _dot(s, w_ref[...].astype(BF16)) + b_ref[...]


def _modulation(cond8, w_mod, b_mod):
    n_col = (N_MOD * D_MODEL) // MOD_COL_TILE
    out = pl.pallas_call(
        _mod_kernel,
        grid=(DEPTH, n_col),
        in_specs=[
            pl.BlockSpec((SUBLANES, D_MODEL), lambda l, j: (0, 0)),
            pl.BlockSpec((None, D_MODEL, MOD_COL_TILE), lambda l, j: (l, 0, j)),
            pl.BlockSpec((None, 1, MOD_COL_TILE), lambda l, j: (l, 0, j)),
        ],
        out_specs=pl.BlockSpec((None, SUBLANES, MOD_COL_TILE), lambda l, j: (l, 0, j)),
        out_shape=jax.ShapeDtypeStruct((DEPTH, SUBLANES, N_MOD * D_MODEL), F32),
        compiler_params=pltpu.CompilerParams(
            dimension_semantics=("arbitrary", "arbitrary"), vmem_limit_bytes=VMEM_LIMIT_BYTES),
        name="modulation",
    )(cond8, w_mod, b_mod.reshape(DEPTH, 1, N_MOD * D_MODEL))
    return out.reshape(DEPTH, SUBLANES, N_MOD, D_MODEL)


def _rope_cols(x, cos, sin, shift, lo_mask):
    fwd = pltpu.roll(x, LANES - shift, axis=1)
    bwd = pltpu.roll(x, shift, axis=1)
    return x * cos + jnp.where(lo_mask, fwd, bwd) * sin


def _inproj_kernel(*refs, latent):
    if latent:
        (x_ref, mod_ref, gpre_ref, win_ref, gqk_ref, seg_ref, cosa_ref, sina_ref, cosb_ref, sinb_ref,
         qkv_ref, cd_ref) = refs
    else:
        (x_ref, mod_ref, gpre_ref, win_ref, gqk_ref, seg_ref, _, _, _, _,
         qkv_ref, cd_ref, oak_ref, oav_ref, obk_ref, obv_ref) = refs
    h = _rms(x_ref[...], gpre_ref[...]) * (1.0 + mod_ref[1:2, :]) + mod_ref[0:1, :]
    p = _dot(h.astype(BF16), win_ref[...])

    pa = p[:, COL_AQ:COL_AV]
    msq = _split_dot(pa * pa, seg_ref[...])
    pa = pa * lax.rsqrt(msq + RMS_EPS) * gqk_ref[...]
    pb = p[:, COL_BQ:COL_BV]
    lane = lax.broadcasted_iota(jnp.int32, (1, LANES), 1)
    if not latent:
        oak_ref[...] = pa[:, COL_AK:COL_AV].reshape(oak_ref.shape)
        oav_ref[...] = p[:, COL_AV:COL_BQ].reshape(oav_ref.shape)
        obk_ref[...] = p[:, COL_BK:COL_BV].reshape(obk_ref.shape)
        obv_ref[...] = p[:, COL_BV:COL_CX].reshape(obv_ref.shape)
        a_cols = [pa[:, c * LANES:(c + 1) * LANES] for c in range(3)]
        b_cols = [pb[:, c * LANES:(c + 1) * LANES] for c in range(4)]
    else:
        lo_a = (lane % 32) < 16
        lo_b = (lane % 16) < 8
        cosa, sina, cosb, sinb = cosa_ref[...], sina_ref[...], cosb_ref[...], sinb_ref[...]
        a_cols = [_rope_cols(pa[:, c * LANES:(c + 1) * LANES], cosa, sina, 16, lo_a) for c in range(3)]
        b_cols = [_rope_cols(pb[:, c * LANES:(c + 1) * LANES], cosb, sinb, 8, lo_b) for c in range(4)]
    a_scale = A_HEAD_DIM ** -0.5 * LOG2E
    b_scale = B_QK_DIM ** -0.5 * LOG2E
    aq0, aq1 = _pair_heads(a_cols[0] * a_scale, a_cols[1] * a_scale, lane < 64)
    qkv_ref[:, QKV_AQ:QKV_AQ + 128] = aq0.astype(BF16)
    qkv_ref[:, QKV_AQ + 128:QKV_AQ + 256] = aq1.astype(BF16)
    qkv_ref[:, QKV_AV:QKV_AV + 128] = p[:, COL_AV:COL_BQ].astype(BF16)
    qkv_ref[:, QKV_AV + 128:QKV_AV + 256] = jnp.ones((p.shape[0], LANES), BF16)
    qkv_ref[:, QKV_BQ:QKV_BQ + 128] = (b_cols[0] * b_scale).astype(BF16)
    qkv_ref[:, QKV_BQ + 128:QKV_BQ + 256] = (b_cols[1] * b_scale).astype(BF16)
    qkv_ref[:, QKV_BK:QKV_BK + 128] = b_cols[2].astype(BF16)
    qkv_ref[:, QKV_BK + 128:QKV_BK + 256] = b_cols[3].astype(BF16)
    qkv_ref[:, QKV_BV:QKV_BV + 256] = p[:, COL_BV:COL_CX].astype(BF16)
    qkv_ref[:, QKV_AK:QKV_AK + 128] = a_cols[2].astype(BF16)
    cd_ref[:, 0:512] = p[:, COL_DZ:IN_WIDTH]
    cd_ref[:, 512:768] = p[:, COL_CX:COL_DZ]


def _inproj(x, mods, layer, gpre, win, gqk, seg, rope, own, *, latent, seq):
    n = x.shape[0]
    tm = TOKEN_TILE
    tiles_per_seq = seq // tm if latent else 1
    if latent:
        mod_row = lambda i: 1 + i // tiles_per_seq
    else:
        mod_row = lambda i: 0
    in_specs = [
        pl.BlockSpec((tm, D_MODEL), lambda i: (i, 0)),
        pl.BlockSpec((None, None, N_MOD, D_MODEL), lambda i: (layer, mod_row(i), 0, 0)),
        pl.BlockSpec((None, 1, D_MODEL), lambda i: (layer, 0, 0)),
        pl.BlockSpec((None, D_MODEL, IN_WIDTH), lambda i: (layer, 0, 0)),
        pl.BlockSpec((None, 1, 384), lambda i: (layer, 0, 0)),
        pl.BlockSpec((384, 384), lambda i: (0, 0)),
    ]
    args = [x, mods, gpre, win, gqk, seg]
    out_specs = [pl.BlockSpec((tm, QKV_WIDTH), lambda i: (i, 0)),
                 pl.BlockSpec((tm, CD_WIDTH), lambda i: (i, 0))]
    out_shape = [jax.ShapeDtypeStruct((n, QKV_WIDTH), BF16), jax.ShapeDtypeStruct((n, CD_WIDTH), F32)]
    aliases = {}
    if latent:
        in_specs += [pl.BlockSpec((tm, LANES), lambda i: (i % tiles_per_seq, 0))] * 4
        args += list(rope)
    else:
        for k, arr in enumerate(own):
            aliases[len(args)] = len(out_shape)
            in_specs.append(pl.BlockSpec(memory_space=pl.ANY))
            args.append(arr)
            out_specs.append(pl.BlockSpec((tm // seq, None, seq, arr.shape[-1]), lambda i: (i, layer, 0, 0)))
            out_shape.append(jax.ShapeDtypeStruct(arr.shape, arr.dtype))
    return pl.pallas_call(
        functools.partial(_inproj_kernel, latent=latent),
        grid=(n // tm,),
        in_specs=in_specs,
        out_specs=out_specs,
        out_shape=out_shape,
        input_output_aliases=aliases,
        compiler_params=pltpu.CompilerParams(
            dimension_semantics=("arbitrary",), vmem_limit_bytes=VMEM_LIMIT_BYTES),
        name="inproj_latent" if latent else "inproj_context",
    )(*args)


def _scores(qm, k_segs):
    return [_dot_nt(qm, k) for k in k_segs]


def _numerators(ss):
    m = functools.reduce(jnp.maximum, [jnp.max(s, axis=-1, keepdims=True) for s in ss])
    return [jnp.exp2(s - m) for s in ss]


def _rowsum(ps):
    return functools.reduce(jnp.add, [jnp.sum(p, axis=-1, keepdims=True) for p in ps])


def _pool_windows(xh, tq):
    n = tq + 2 * HALO
    up = lambda a, k: pltpu.roll(a, n - k, axis=0)
    a2 = xh + up(xh, 1)
    a4 = a2 + up(a2, 2)
    a8 = a4 + up(a4, 4)
    a16 = a8 + up(a8, 8)
    w2 = up(a2, HALO - 1)[0:tq]
    w4 = up(a4, HALO - 2)[0:tq]
    w8 = up(a8, HALO - 4)[0:tq]
    w16 = a16[0:tq]
    return w2, w4, w8, w16


def _mixer_kernel(*refs, latent, lam_init, tq, n_q):
    if latent:
        (aq_ref, ak_ref, av_ref, bq_ref, bk_ref, bv_ref, cak_ref, cav_ref, cbk_ref, cbv_ref,
         cx_ref, cxp_ref, cxn_ref, dz_ref, icnt_ref, lamp_ref, subln_ref, cw_ref, cscale_ref,
         gv_ref, ws_ref, bs_ref, out_ref) = refs
    else:
        (aq_ref, ak_ref, av_ref, bq_ref, bk_ref, bv_ref,
         cx_ref, cxp_ref, cxn_ref, dz_ref, icnt_ref, lamp_ref, subln_ref, cw_ref, cscale_ref,
         gv_ref, ws_ref, bs_ref, out_ref) = refs
        cak_ref = cav_ref = cbk_ref = cbv_ref = None
    i = pl.program_id(1)
    lane = lax.broadcasted_iota(jnp.int32, (1, LANES), 1)
    half0 = lane < 64

    prev = jnp.where(i == 0, 0.0, cxp_ref[...])
    nxt = jnp.where(i == n_q - 1, 0.0, cxn_ref[...])
    cur = cx_ref[...]
    xh = jnp.concatenate([prev, cur, nxt], axis=0)
    w2, w4, _, _ = _pool_windows(xh[:, 0:LANES], tq)
    _, _, w8, w16 = _pool_windows(xh[:, LANES:2 * LANES], tq)
    sums = jnp.concatenate([jnp.where(half0, w2, w4), jnp.where(half0, w8, w16)], axis=1)
    pooled = sums * icnt_ref[...] - cur
    yc = _dot(pooled.astype(BF16), cw_ref[...]) * cscale_ref[...]
    out_ref[:, 512:768] = yc.astype(BF16)

    z = jax.nn.gelu(dz_ref[...])
    u_gate = z[:, 0:GROUP_W]
    v = _rms(z[:, GROUP_W:2 * GROUP_W], gv_ref[...])
    lane2 = lax.broadcasted_iota(jnp.int32, (1, GROUP_W), 1)
    for n in range(tq // CHUNK):
        vc = v[n * CHUNK:(n + 1) * CHUNK]
        stacked = jnp.concatenate(
            [jnp.where((lane2 >= 64 * g) & (lane2 < 64 * (g + 1)), vc, 0.0).astype(BF16) for g in range(4)],
            axis=0)
        sg = _dot(ws_ref[...], stacked) + bs_ref[...]
        out_ref[n * CHUNK:(n + 1) * CHUNK, 768:1024] = (u_gate[n * CHUNK:(n + 1) * CHUNK] * sg).astype(BF16)

    lp = lamp_ref[...]
    lam = (jnp.exp(jnp.sum(lp[0:1] * lp[1:2], axis=-1, keepdims=True))
           - jnp.exp(jnp.sum(lp[2:3] * lp[3:4], axis=-1, keepdims=True)) + lam_init)

    def key_segments(ctx_ref, own_ref, cols):
        return ([ctx_ref[:, cols]] if latent else []) + [own_ref[:, cols]]

    a_k = key_segments(cak_ref, ak_ref, slice(0, LANES))
    a_v = key_segments(cav_ref, av_ref, slice(0, 2 * LANES))
    b_v = key_segments(cbv_ref, bv_ref, slice(0, 2 * LANES))

    def head_queries(t):
        kind, c, u = heads[t]
        q_ref = aq_ref if kind == "a" else bq_ref
        qc = q_ref[:, c * LANES:(c + 1) * LANES]
        zero = jnp.zeros_like(qc)
        if kind == "a":
            return [jnp.where(half0 if u == 0 else ~half0, qc, zero)]
        sels = [(lane >= 64 * u + 32 * j) & (lane < 64 * u + 32 * j + 32) for j in range(2)]
        return [jnp.where(sel, qc, zero) for sel in sels]

    def head_scores(t):
        kind, c, _ = heads[t]
        k_segs = a_k if kind == "a" else key_segments(cbk_ref, bk_ref, slice(c * LANES, (c + 1) * LANES))
        return [_scores(qm, k_segs) for qm in head_queries(t)]

    def head_output(t, score_sets):
        kind, c, u = heads[t]
        if kind == "a":
            ps = _numerators(score_sets[0])
            o = functools.reduce(jnp.add, [_dot(p.astype(BF16), v) for p, v in zip(ps, a_v)])
            return o[:, 0:LANES] / o[:, LANES:2 * LANES]
        p0, p1 = _numerators(score_sets[0]), _numerators(score_sets[1])
        c0 = 1.0 / _rowsum(p0)
        c1 = lam / _rowsum(p1)
        o = functools.reduce(jnp.add, [
            _dot((a * c0 - b * c1).astype(BF16), v) for a, b, v in zip(p0, p1, b_v)])
        o = o[:, c * LANES:(c + 1) * LANES]
        valid = half0 if u == 0 else ~half0
        ms = jnp.sum(jnp.where(valid, o * o, 0.0), axis=-1, keepdims=True) * (1.0 / 64)
        return o * lax.rsqrt(ms + RMS_EPS)

    heads = [(kind, c, u) for kind in ("a", "b") for c in range(2) for u in range(2)]
    outs = []
    pending = head_scores(0)
    for t in range(len(heads)):
        score_sets = pending
        if t + 1 < len(heads):
            pending = head_scores(t + 1)
        outs.append(head_output(t, score_sets))
    a0, a1 = _unpair_heads(jnp.where(half0, outs[0], outs[1]), jnp.where(half0, outs[2], outs[3]), half0)
    out_ref[:, 0:128] = a0.astype(BF16)
    out_ref[:, 128:256] = a1.astype(BF16)
    b_gain = subln_ref[...] * (1.0 - lam_init)
    out_ref[:, 256:384] = (jnp.where(half0, outs[4], outs[5]) * b_gain).astype(BF16)
    out_ref[:, 384:512] = (jnp.where(half0, outs[6], outs[7]) * b_gain).astype(BF16)


def _mixers(qkv, cd, ctx, icnt, lamp, subln, cw, cscale, gv, ws, bs, layer, *, latent, seq, lam_init):
    n = qkv.shape[0]
    n_batch = n // seq
    tq = Q_TILE
    n_q = seq // tq
    row = lambda b, i: b * n_q + i
    row8 = tq // SUBLANES
    last8 = n // SUBLANES - 1
    in_specs = [
        pl.BlockSpec((tq, 256), lambda b, i: (row(b, i), QKV_AQ // 256)),
        pl.BlockSpec((seq, 128), lambda b, i: (b, QKV_AK // 128)),
        pl.BlockSpec((seq, 256), lambda b, i: (b, QKV_AV // 256)),
        pl.BlockSpec((tq, 256), lambda b, i: (row(b, i), QKV_BQ // 256)),
        pl.BlockSpec((seq, 256), lambda b, i: (b, QKV_BK // 256)),
        pl.BlockSpec((seq, 256), lambda b, i: (b, QKV_BV // 256)),
    ]
    args = [qkv] * 6
    if latent:
        for w in (128, 256, 256, 256):
            in_specs.append(pl.BlockSpec((None, None, 256, w), lambda b, i: (b, layer, 0, 0)))
        args += list(ctx)
    in_specs += [
        pl.BlockSpec((tq, 256), lambda b, i: (row(b, i), 2)),
        pl.BlockSpec((HALO, 256), lambda b, i: (jnp.maximum(row(b, i) * row8 - 1, 0), 2)),
        pl.BlockSpec((HALO, 256), lambda b, i: (jnp.minimum((row(b, i) + 1) * row8, last8), 2)),
        pl.BlockSpec((tq, 512), lambda b, i: (row(b, i), 0)),
        pl.BlockSpec((tq, 256), lambda b, i: (i, 0)),
        pl.BlockSpec((None, SUBLANES, LANES), lambda b, i: (layer, 0, 0)),
        pl.BlockSpec((None, 1, LANES), lambda b, i: (layer, 0, 0)),
        pl.BlockSpec((None, 256, 256), lambda b, i: (layer, 0, 0)),
        pl.BlockSpec((None, 1, 256), lambda b, i: (layer, 0, 0)),
        pl.BlockSpec((None, 1, 256), lambda b, i: (layer, 0, 0)),
        pl.BlockSpec((None, CHUNK, 4 * CHUNK), lambda b, i: (layer, 0, 0)),
        pl.BlockSpec((None, CHUNK, 256), lambda b, i: (layer, 0, 0)),
    ]
    args += [cd, cd, cd, cd, icnt, lamp, subln, cw, cscale, gv, ws, bs]
    return pl.pallas_call(
        functools.partial(_mixer_kernel, latent=latent, lam_init=lam_init, tq=tq, n_q=n_q),
        grid=(n_batch, n_q),
        in_specs=in_specs,
        out_specs=pl.BlockSpec((tq, D_MODEL), lambda b, i: (row(b, i), 0)),
        out_shape=jax.ShapeDtypeStruct((n, D_MODEL), BF16),
        compiler_params=pltpu.CompilerParams(
            dimension_semantics=("arbitrary", "arbitrary"), vmem_limit_bytes=VMEM_LIMIT_BYTES),
        name="mixers_latent" if latent else "mixers_context",
    )(*args)


def _ffn_kernel(mix_ref, x_ref, mod_ref, wout_ref, gpm_ref, gpf_ref, gpo_ref, wg_ref, wu_ref, wd_ref,
                out_ref, acc_ref):
    y = _dot(mix_ref[...], wout_ref[...])
    x1 = x_ref[...] + mod_ref[2:3, :] * _rms(y, gpm_ref[...])
    h = (_rms(x1, gpf_ref[...]) * (1.0 + mod_ref[4:5, :]) + mod_ref[3:4, :]).astype(BF16)
    for c in range(N_FF_CHUNKS):
        cols = slice(c * FF_CHUNK, (c + 1) * FF_CHUNK)
        g = _dot(h, wg_ref[:, cols])
        u = _dot(h, wu_ref[:, cols])
        part = _dot((jax.nn.silu(g) * u).astype(BF16), wd_ref[cols, :])
        if c == 0:
            acc_ref[...] = part
        else:
            acc_ref[...] += part
    out_ref[...] = x1 + mod_ref[5:6, :] * _rms(acc_ref[...], gpo_ref[...])


def _outproj_ffn(mixed, x, mods, layer, wout, gpm, gpf, gpo, wg, wu, wd, *, latent, seq):
    n = x.shape[0]
    tm = TOKEN_TILE
    tiles_per_seq = seq // tm if latent else 1
    if latent:
        mod_row = lambda i: 1 + i // tiles_per_seq
    else:
        mod_row = lambda i: 0
    const = dict(pipeline_mode=pl.Buffered(1))
    gain = lambda: pl.BlockSpec((None, 1, D_MODEL), lambda i: (layer, 0, 0))
    return pl.pallas_call(
        _ffn_kernel,
        grid=(n // tm,),
        in_specs=[
            pl.BlockSpec((tm, D_MODEL), lambda i: (i, 0)),
            pl.BlockSpec((tm, D_MODEL), lambda i: (i, 0)),
            pl.BlockSpec((None, None, N_MOD, D_MODEL), lambda i: (layer, mod_row(i), 0, 0)),
            pl.BlockSpec((None, D_MODEL, D_MODEL), lambda i: (layer, 0, 0), **const),
            gain(), gain(), gain(),
            pl.BlockSpec((None, D_MODEL, D_FF), lambda i: (layer, 0, 0), **const),
            pl.BlockSpec((None, D_MODEL, D_FF), lambda i: (layer, 0, 0), **const),
            pl.BlockSpec((None, D_FF, D_MODEL), lambda i: (layer, 0, 0), **const),
        ],
        out_specs=pl.BlockSpec((tm, D_MODEL), lambda i: (i, 0)),
        out_shape=jax.ShapeDtypeStruct((n, D_MODEL), F32),
        scratch_shapes=[pltpu.VMEM((tm, D_MODEL), F32)],
        compiler_params=pltpu.CompilerParams(
            dimension_semantics=("arbitrary",), vmem_limit_bytes=VMEM_LIMIT_BYTES),
        name="outproj_ffn_latent" if latent else "outproj_ffn_context",
    )(mixed, x, mods, wout, gpm, gpf, gpo, wg, wu, wd)


def _rope_tables(seq):
    t = np.arange(seq)
    pos = {"row": (t // GRID_W).astype(np.float64), "col": (t % GRID_W).astype(np.float64)}

    def table(head_dim):
        half = head_dim // 2
        inv_freq = ROPE_THETA ** (-np.arange(0, half, 2, dtype=np.float64) / half)
        lane = np.arange(head_dim)
        within = lane % half
        freq = inv_freq[within % (half // 2)]
        p = np.where(lane[None, :] < half, pos["row"][:, None], pos["col"][:, None])
        ang = p * freq[None, :]
        sign = np.where(within < half // 2, -1.0, 1.0)
        reps = LANES // head_dim
        return (np.tile(np.cos(ang), (1, reps)).astype(np.float32),
                np.tile(np.sin(ang) * sign[None, :], (1, reps)).astype(np.float32))

    cosa, sina = table(A_HEAD_DIM)
    cosb, sinb = table(B_QK_DIM)
    return cosa, sina, cosb, sinb


def _inv_count_table(seq):
    t = np.arange(seq)
    cols = []
    for win in POOL_WINDOWS:
        cnt = np.minimum(t + win // 2, seq) - np.maximum(t - win // 2, 0)
        cols.append(np.repeat((1.0 / cnt)[:, None], 64, axis=1))
    return np.concatenate(cols, axis=1).astype(np.float32)


def _segment_mean_matrix():
    idx = np.arange(384) // A_HEAD_DIM
    return (idx[:, None] == idx[None, :]).astype(np.float32) / A_HEAD_DIM


def kernel(x_prompt, x_sample, cache_a_k, cache_a_v, cache_b_k, cache_b_v, c, c_ctx, w_mod, b_mod, g_pre_mix, g_post_mix, g_pre_ffn, g_post_ffn, w_in, w_out, a_q_norm, a_k_norm, b_lq1, b_lk1, b_lq2, b_lk2, b_subln, c_w, c_scale, d_v_norm, d_ws, d_bs, w_gate, w_up, w_down):
    batch, seq, _ = x_prompt.shape
    dec_batch, dec_seq, _ = x_sample.shape
    past = cache_a_k.shape[2]

    win = w_in.astype(BF16)
    wout = w_out.astype(BF16)
    wg = w_gate.astype(BF16)
    wu = w_up.astype(BF16)
    wd = w_down.astype(BF16)
    gqk = jnp.concatenate([jnp.tile(a_q_norm, (1, 4)), jnp.tile(a_k_norm, (1, 2))], axis=1).reshape(DEPTH, 1, 384)
    seg = jnp.asarray(_segment_mean_matrix(), BF16)
    lamp = jnp.stack([b_lq1, b_lk1, b_lq2, b_lk2], axis=1)
    lamp = jnp.pad(lamp, ((0, 0), (0, SUBLANES - 4), (0, LANES - B_QK_DIM)))
    subln = jnp.tile(b_subln, (1, 2)).reshape(DEPTH, 1, LANES)
    eye = jnp.eye(4, dtype=F32)
    cw = (eye[None, :, None, :, None] * c_w[:, :, :, None, :]).reshape(DEPTH, 256, 256).astype(BF16)
    cscale = c_scale.reshape(DEPTH, 1, GROUP_W)
    gv = d_v_norm.reshape(DEPTH, 1, GROUP_W)
    ws = d_ws.transpose(0, 2, 1, 3).reshape(DEPTH, CHUNK, 4 * CHUNK).astype(BF16)
    bs = jnp.repeat(d_bs.transpose(0, 2, 1), 64, axis=2)
    gpre = g_pre_mix.reshape(DEPTH, 1, D_MODEL)
    gpm = g_post_mix.reshape(DEPTH, 1, D_MODEL)
    gpf = g_pre_ffn.reshape(DEPTH, 1, D_MODEL)
    gpo = g_post_ffn.reshape(DEPTH, 1, D_MODEL)
    rope = tuple(jnp.asarray(t) for t in _rope_tables(dec_seq))
    icnt_ctx = jnp.asarray(_inv_count_table(seq))
    icnt_lat = jnp.asarray(_inv_count_table(dec_seq))
    flat = lambda a, w: a.reshape(dec_batch, DEPTH, past, w).astype(BF16)
    ctx = (flat(cache_a_k, 128),
           jnp.concatenate([flat(cache_a_v, 128), jnp.ones((dec_batch, DEPTH, past, 128), BF16)], axis=-1),
           flat(cache_b_k, 256), flat(cache_b_v, 256))

    cond8 = jnp.concatenate([c_ctx[None, :], c, jnp.zeros((SUBLANES - 1 - dec_batch, D_MODEL), F32)], axis=0)
    mods = _modulation(cond8, w_mod, b_mod)

    yp = x_prompt.reshape(batch * seq, D_MODEL)
    ys = x_sample.reshape(dec_batch * dec_seq, D_MODEL)
    own = [jnp.zeros((batch, DEPTH, seq, w), F32) for w in (128, 128, 256, 256)]
    for layer in range(DEPTH):
        lam_init = 0.8 - 0.6 * math.exp(-0.3 * layer)
        mix_args = (lamp, subln, cw, cscale, gv, ws, bs, layer)
        ffn_args = (mods, layer, wout, gpm, gpf, gpo, wg, wu, wd)
        qkv, cd, *own = _inproj(yp, mods, layer, gpre, win, gqk, seg, None, own, latent=False, seq=seq)
        mixed = _mixers(qkv, cd, None, icnt_ctx, *mix_args, latent=False, seq=seq, lam_init=lam_init)
        yp = _outproj_ffn(mixed, yp, *ffn_args, latent=False, seq=seq)
        qkv, cd = _inproj(ys, mods, layer, gpre, win, gqk, seg, rope, None, latent=True, seq=dec_seq)
        mixed = _mixers(qkv, cd, ctx, icnt_lat, *mix_args, latent=True, seq=dec_seq, lam_init=lam_init)
        ys = _outproj_ffn(mixed, ys, *ffn_args, latent=True, seq=dec_seq)

    new_a_k, new_a_v, new_b_k, new_b_v = own
    lead = (batch, DEPTH, seq)
    return (yp.reshape(batch, seq, D_MODEL), ys.reshape(dec_batch, dec_seq, D_MODEL),
            new_a_k.reshape(lead + (2, 64)), new_a_v.reshape(lead + (2, 64)),
            new_b_k.reshape(lead + (4, 2, 32)), new_b_v.reshape(lead + (4, 64)))
```

```python
import functools
import math

import numpy as np
import jax
import jax.numpy as jnp
from jax import lax
from jax.experimental import pallas as pl
from jax.experimental.pallas import tpu as pltpu

F32 = jnp.float32
BF16 = jnp.bfloat16

D_MODEL = 1024
DEPTH = 4
GRID_W = 64
GROUP_W = 256
A_HEAD_DIM = 64
B_QK_DIM = 32
POOL_WINDOWS = (2, 4, 8, 16)
CHUNK = 128
D_FF = 2816
N_MOD = 6
RMS_EPS = 1e-6
LOG2E = math.log2(math.e)
ROPE_THETA = 10000.0
IN_WIDTH = 2048

LANES = 128
SUBLANES = 8
VMEM_LIMIT_BYTES = 56 * 1024 * 1024

COL_AQ, COL_AK, COL_AV, COL_BQ, COL_BK, COL_BV, COL_CX, COL_DZ = 0, 256, 384, 512, 768, 1024, 1280, 1536
QKV_AQ, QKV_AV, QKV_BQ, QKV_BK, QKV_BV, QKV_AK = 0, 256, 512, 768, 1024, 1280
QKV_WIDTH = 1408
CD_WIDTH = 768
FF_CHUNK = 256
N_FF_CHUNKS = D_FF // FF_CHUNK

TOKEN_TILE = 512
FFN_TOKEN_TILE = 512
Q_TILE = 512
CONTEXT_SEQS_PER_STEP = 4
MOD_COL_TILE = 1536
HALO = 8


def _rms(x, g):
    ms = jnp.mean(x * x, axis=-1, keepdims=True)
    return x * lax.rsqrt(ms + RMS_EPS) * g


def _dot(a, b):
    return jnp.dot(a, b, preferred_element_type=F32)


def _dot_nt(a, b):
    return lax.dot_general(a, b, (((1,), (1,)), ((), ())), preferred_element_type=F32)


def _pair_heads(c0, c1, half0):
    return jnp.where(half0, c0, c1), pltpu.roll(jnp.where(half0, c1, c0), 64, axis=1)


def _unpair_heads(d0, d1, half0):
    r = pltpu.roll(d1, 64, axis=1)
    return jnp.where(half0, d0, r), jnp.where(half0, r, d0)


def _split_dot(x, w):
    hi = x.astype(BF16)
    lo = (x - hi.astype(F32)).astype(BF16)
    return _dot(hi, w) + _dot(lo, w)


def _mod_kernel(cond_ref, w_ref, b_ref, out_ref):
    s = jax.nn.silu(cond_ref[...]).astype(BF16)
    out_ref[...] = _dot(s, w_ref[...].astype(BF16)) + b_ref[...]


def _modulation(cond8, w_mod, b_mod):
    n_col = (N_MOD * D_MODEL) // MOD_COL_TILE
    out = pl.pallas_call(
        _mod_kernel,
        grid=(DEPTH, n_col),
        in_specs=[
            pl.BlockSpec((SUBLANES, D_MODEL), lambda l, j: (0, 0)),
            pl.BlockSpec((None, D_MODEL, MOD_COL_TILE), lambda l, j: (l, 0, j)),
            pl.BlockSpec((None, 1, MOD_COL_TILE), lambda l, j: (l, 0, j)),
        ],
        out_specs=pl.BlockSpec((None, SUBLANES, MOD_COL_TILE), lambda l, j: (l, 0, j)),
        out_shape=jax.ShapeDtypeStruct((DEPTH, SUBLANES, N_MOD * D_MODEL), F32),
        compiler_params=pltpu.CompilerParams(
            dimension_semantics=("arbitrary", "arbitrary"), vmem_limit_bytes=VMEM_LIMIT_BYTES),
        name="modulation",
    )(cond8, w_mod, b_mod.reshape(DEPTH, 1, N_MOD * D_MODEL))
    return out.reshape(DEPTH, SUBLANES, N_MOD, D_MODEL)


def _rope_cols(x, cos, sin, shift, lo_mask):
    fwd = pltpu.roll(x, LANES - shift, axis=1)
    bwd = pltpu.roll(x, shift, axis=1)
    return x * cos + jnp.where(lo_mask, fwd, bwd) * sin


def _inproj_kernel(*refs, latent):
    if latent:
        (x_ref, mod_ref, gpre_ref, win_ref, gqk_ref, seg_ref, cosa_ref, sina_ref, cosb_ref, sinb_ref,
         qkv_ref, cd_ref) = refs
    else:
        (x_ref, mod_ref, gpre_ref, win_ref, gqk_ref, seg_ref, _, _, _, _,
         qkv_ref, cd_ref, oak_ref, oav_ref, obk_ref, obv_ref) = refs
    h = _rms(x_ref[...], gpre_ref[...]) * (1.0 + mod_ref[1:2, :]) + mod_ref[0:1, :]
    p = _dot(h.astype(BF16), win_ref[...])

    pa = p[:, COL_AQ:COL_AV]
    msq = _split_dot(pa * pa, seg_ref[...])
    pa = pa * lax.rsqrt(msq + RMS_EPS) * gqk_ref[...]
    pb = p[:, COL_BQ:COL_BV]
    lane = lax.broadcasted_iota(jnp.int32, (1, LANES), 1)
    if not latent:
        oak_ref[...] = pa[:, COL_AK:COL_AV].reshape(oak_ref.shape)
        oav_ref[...] = p[:, COL_AV:COL_BQ].reshape(oav_ref.shape)
        obk_ref[...] = p[:, COL_BK:COL_BV].reshape(obk_ref.shape)
        obv_ref[...] = p[:, COL_BV:COL_CX].reshape(obv_ref.shape)
        a_cols = [pa[:, c * LANES:(c + 1) * LANES] for c in range(3)]
        b_cols = [pb[:, c * LANES:(c + 1) * LANES] for c in range(4)]
    else:
        lo_a = (lane % 32) < 16
        lo_b = (lane % 16) < 8
        cosa, sina, cosb, sinb = cosa_ref[...], sina_ref[...], cosb_ref[...], sinb_ref[...]
        a_cols = [_rope_cols(pa[:, c * LANES:(c + 1) * LANES], cosa, sina, 16, lo_a) for c in range(3)]
        b_cols = [_rope_cols(pb[:, c * LANES:(c + 1) * LANES], cosb, sinb, 8, lo_b) for c in range(4)]
    a_scale = A_HEAD_DIM ** -0.5 * LOG2E
    b_scale = B_QK_DIM ** -0.5 * LOG2E
    aq0, aq1 = _pair_heads(a_cols[0] * a_scale, a_cols[1] * a_scale, lane < 64)
    qkv_ref[:, QKV_AQ:QKV_AQ + 128] = aq0.astype(BF16)
    qkv_ref[:, QKV_AQ + 128:QKV_AQ + 256] = aq1.astype(BF16)
    qkv_ref[:, QKV_AV:QKV_AV + 128] = p[:, COL_AV:COL_BQ].astype(BF16)
    qkv_ref[:, QKV_AV + 128:QKV_AV + 256] = jnp.ones((p.shape[0], LANES), BF16)
    qkv_ref[:, QKV_BQ:QKV_BQ + 128] = (b_cols[0] * b_scale).astype(BF16)
    qkv_ref[:, QKV_BQ + 128:QKV_BQ + 256] = (b_cols[1] * b_scale).astype(BF16)
    qkv_ref[:, QKV_BK:QKV_BK + 128] = b_cols[2].astype(BF16)
    qkv_ref[:, QKV_BK + 128:QKV_BK + 256] = b_cols[3].astype(BF16)
    qkv_ref[:, QKV_BV:QKV_BV + 256] = p[:, COL_BV:COL_CX].astype(BF16)
    qkv_ref[:, QKV_AK:QKV_AK + 128] = a_cols[2].astype(BF16)
    cd_ref[:, 0:512] = p[:, COL_DZ:IN_WIDTH]
    cd_ref[:, 512:768] = p[:, COL_CX:COL_DZ]


def _inproj(x, mods, layer, gpre, win, gqk, seg, rope, own, *, latent, seq):
    n = x.shape[0]
    tm = TOKEN_TILE
    tiles_per_seq = seq // tm if latent else 1
    if latent:
        mod_row = lambda i: 1 + i // tiles_per_seq
    else:
        mod_row = lambda i: 0
    in_specs = [
        pl.BlockSpec((tm, D_MODEL), lambda i: (i, 0)),
        pl.BlockSpec((None, None, N_MOD, D_MODEL), lambda i: (layer, mod_row(i), 0, 0)),
        pl.BlockSpec((None, 1, D_MODEL), lambda i: (layer, 0, 0)),
        pl.BlockSpec((None, D_MODEL, IN_WIDTH), lambda i: (layer, 0, 0)),
        pl.BlockSpec((None, 1, 384), lambda i: (layer, 0, 0)),
        pl.BlockSpec((384, 384), lambda i: (0, 0)),
    ]
    args = [x, mods, gpre, win, gqk, seg]
    out_specs = [pl.BlockSpec((tm, QKV_WIDTH), lambda i: (i, 0)),
                 pl.BlockSpec((tm, CD_WIDTH), lambda i: (i, 0))]
    out_shape = [jax.ShapeDtypeStruct((n, QKV_WIDTH), BF16), jax.ShapeDtypeStruct((n, CD_WIDTH), F32)]
    aliases = {}
    if latent:
        in_specs += [pl.BlockSpec((tm, LANES), lambda i: (i % tiles_per_seq, 0))] * 4
        args += list(rope)
    else:
        for arr in own:
            aliases[len(args)] = len(out_shape)
            in_specs.append(pl.BlockSpec(memory_space=pl.ANY))
            args.append(arr)
            out_specs.append(pl.BlockSpec((tm // seq, None, seq, arr.shape[-1]), lambda i: (i, layer, 0, 0)))
            out_shape.append(jax.ShapeDtypeStruct(arr.shape, arr.dtype))
    return pl.pallas_call(
        functools.partial(_inproj_kernel, latent=latent),
        grid=(n // tm,),
        in_specs=in_specs,
        out_specs=out_specs,
        out_shape=out_shape,
        input_output_aliases=aliases,
        compiler_params=pltpu.CompilerParams(
            dimension_semantics=("arbitrary",), vmem_limit_bytes=VMEM_LIMIT_BYTES),
        name="inproj_latent" if latent else "inproj_context",
    )(*args)


def _scores(qm, k_segs):
    return [_dot_nt(qm, k) for k in k_segs]


def _numerators(ss):
    m = functools.reduce(jnp.maximum, [jnp.max(s, axis=-1, keepdims=True) for s in ss])
    return [jnp.exp2(s - m) for s in ss]


def _rowsum(ps):
    return functools.reduce(jnp.add, [jnp.sum(p, axis=-1, keepdims=True) for p in ps])


def _pool_windows(xh, tq):
    n = tq + 2 * HALO
    up = lambda a, k: pltpu.roll(a, n - k, axis=0)
    a2 = xh + up(xh, 1)
    a4 = a2 + up(a2, 2)
    a8 = a4 + up(a4, 4)
    a16 = a8 + up(a8, 8)
    w2 = up(a2, HALO - 1)[0:tq]
    w4 = up(a4, HALO - 2)[0:tq]
    w8 = up(a8, HALO - 4)[0:tq]
    w16 = a16[0:tq]
    return w2, w4, w8, w16


def _mixer_kernel(*refs, latent, lam_init, tq, n_q, n_sub, seq, ahead):
    if latent:
        (aq_ref, ak_ref, av_ref, bq_ref, bk_ref, bv_ref, cak_ref, cav_ref, cbk_ref, cbv_ref,
         cx_ref, cxp_ref, cxn_ref, dz_ref, icnt_ref, lamp_ref, subln_ref, cw_ref, cscale_ref,
         gv_ref, ws_ref, bs_ref, out_ref) = refs
    else:
        (aq_ref, ak_ref, av_ref, bq_ref, bk_ref, bv_ref,
         cx_ref, dz_ref, icnt_ref, lamp_ref, subln_ref, cw_ref, cscale_ref,
         gv_ref, ws_ref, bs_ref, out_ref) = refs
        cak_ref = cav_ref = cbk_ref = cbv_ref = cxp_ref = cxn_ref = None
    i = pl.program_id(1)
    lane = lax.broadcasted_iota(jnp.int32, (1, LANES), 1)
    half0 = lane < 64

    for r in range(n_sub):
        rows = slice(r * tq, (r + 1) * tq)
        cur = cx_ref[rows, :]
        if latent:
            prev = jnp.where(i == 0, 0.0, cxp_ref[...])
            nxt = jnp.where(i == n_q - 1, 0.0, cxn_ref[...])
        else:
            prev = nxt = jnp.zeros((HALO, GROUP_W), F32)
        xh = jnp.concatenate([prev, cur, nxt], axis=0)
        w2, w4, _, _ = _pool_windows(xh[:, 0:LANES], tq)
        _, _, w8, w16 = _pool_windows(xh[:, LANES:2 * LANES], tq)
        sums = jnp.concatenate([jnp.where(half0, w2, w4), jnp.where(half0, w8, w16)], axis=1)
        pooled = sums * icnt_ref[...] - cur
        yc = _dot(pooled.astype(BF16), cw_ref[...]) * cscale_ref[...]
        out_ref[rows, 512:768] = yc.astype(BF16)

    lane2 = lax.broadcasted_iota(jnp.int32, (1, GROUP_W), 1)
    for n in range(n_sub * tq // CHUNK):
        rows = slice(n * CHUNK, (n + 1) * CHUNK)
        z = jax.nn.gelu(dz_ref[rows, :])
        vc = _rms(z[:, GROUP_W:2 * GROUP_W], gv_ref[...])
        stacked = jnp.concatenate(
            [jnp.where((lane2 >= 64 * g) & (lane2 < 64 * (g + 1)), vc, 0.0).astype(BF16) for g in range(4)],
            axis=0)
        sg = _dot(ws_ref[...], stacked) + bs_ref[...]
        out_ref[rows, 768:1024] = (z[:, 0:GROUP_W] * sg).astype(BF16)

    lp = lamp_ref[...]
    lam = (jnp.exp(jnp.sum(lp[0:1] * lp[1:2], axis=-1, keepdims=True))
           - jnp.exp(jnp.sum(lp[2:3] * lp[3:4], axis=-1, keepdims=True)) + lam_init)

    def key_segments(ctx_ref, own_ref, r, cols):
        return ([ctx_ref[:, cols]] if latent else []) + [own_ref[r * seq:(r + 1) * seq, cols]]

    def head_queries(t):
        r, kind, c, u = heads[t]
        q_ref = aq_ref if kind == "a" else bq_ref
        qc = q_ref[r * tq:(r + 1) * tq, c * LANES:(c + 1) * LANES]
        zero = jnp.zeros_like(qc)
        if kind == "a":
            return [jnp.where(half0 if u == 0 else ~half0, qc, zero)]
        sels = [(lane >= 64 * u + 32 * j) & (lane < 64 * u + 32 * j + 32) for j in range(2)]
        return [jnp.where(sel, qc, zero) for sel in sels]

    def head_scores(t):
        r, kind, c, _ = heads[t]
        if kind == "a":
            k_segs = key_segments(cak_ref, ak_ref, r, slice(0, LANES))
        else:
            k_segs = key_segments(cbk_ref, bk_ref, r, slice(c * LANES, (c + 1) * LANES))
        return [_scores(qm, k_segs) for qm in head_queries(t)]

    def head_weights(t, score_sets):
        kind = heads[t][1]
        if kind == "a":
            return [p.astype(BF16) for p in _numerators(score_sets[0])], None
        p0, p1 = _numerators(score_sets[0]), _numerators(score_sets[1])
        l0 = _rowsum(p0)
        ratio = lam * l0 / _rowsum(p1)
        return [(a - b * ratio).astype(BF16) for a, b in zip(p0, p1)], l0

    def head_output(t, weights):
        r, kind, c, u = heads[t]
        ws, l0 = weights
        if kind == "a":
            a_v = key_segments(cav_ref, av_ref, r, slice(0, 2 * LANES))
            o = functools.reduce(jnp.add, [_dot(w, v) for w, v in zip(ws, a_v)])
            return o[:, 0:LANES] / o[:, LANES:2 * LANES]
        b_v = key_segments(cbv_ref, bv_ref, r, slice(0, 2 * LANES))
        o = functools.reduce(jnp.add, [_dot(w, v) for w, v in zip(ws, b_v)])
        o = o[:, c * LANES:(c + 1) * LANES] / l0
        valid = half0 if u == 0 else ~half0
        ms = jnp.sum(jnp.where(valid, o * o, 0.0), axis=-1, keepdims=True) * (1.0 / 64)
        return o * lax.rsqrt(ms + RMS_EPS)

    heads = [(r, kind, c, u) for r in range(n_sub) for c in range(2) for u in range(2) for kind in ("a", "b")]
    outs = {}
    if ahead is None:
        scores = [head_scores(t) for t in range(len(heads))]
        weights = [head_weights(t, scores[t]) for t in range(len(heads))]
        for t in range(len(heads)):
            outs[heads[t]] = head_output(t, weights[t])
    else:
        pending = [head_scores(t) for t in range(min(ahead, len(heads)))]
        for t in range(len(heads)):
            if t + ahead < len(heads):
                pending.append(head_scores(t + ahead))
            outs[heads[t]] = head_output(t, head_weights(t, pending.pop(0)))
    b_gain = subln_ref[...] * (1.0 - lam_init)
    for r in range(n_sub):
        rows = slice(r * tq, (r + 1) * tq)
        col = lambda kind, c: jnp.where(half0, outs[(r, kind, c, 0)], outs[(r, kind, c, 1)])
        a0, a1 = _unpair_heads(col("a", 0), col("a", 1), half0)
        out_ref[rows, 0:128] = a0.astype(BF16)
        out_ref[rows, 128:256] = a1.astype(BF16)
        out_ref[rows, 256:384] = (col("b", 0) * b_gain).astype(BF16)
        out_ref[rows, 384:512] = (col("b", 1) * b_gain).astype(BF16)


def _mixers(qkv, cd, ctx, icnt, lamp, subln, cw, cscale, gv, ws, bs, layer, *, latent, seq, lam_init):
    n = qkv.shape[0]
    n_batch = n // seq
    tq = min(Q_TILE, seq)
    n_q = seq // tq
    n_sub = 1 if latent else CONTEXT_SEQS_PER_STEP
    rows_q = n_sub * tq
    rows_k = n_sub * seq
    row = lambda b, i: b * n_q + i
    row8 = tq // SUBLANES
    last8 = n // SUBLANES - 1
    in_specs = [
        pl.BlockSpec((rows_q, 256), lambda b, i: (row(b, i), QKV_AQ // 256)),
        pl.BlockSpec((rows_k, 128), lambda b, i: (b, QKV_AK // 128)),
        pl.BlockSpec((rows_k, 256), lambda b, i: (b, QKV_AV // 256)),
        pl.BlockSpec((rows_q, 256), lambda b, i: (row(b, i), QKV_BQ // 256)),
        pl.BlockSpec((rows_k, 256), lambda b, i: (b, QKV_BK // 256)),
        pl.BlockSpec((rows_k, 256), lambda b, i: (b, QKV_BV // 256)),
    ]
    args = [qkv] * 6
    if latent:
        for w in (128, 256, 256, 256):
            in_specs.append(pl.BlockSpec((None, None, 256, w), lambda b, i: (b, layer, 0, 0)))
        args += list(ctx)
    in_specs.append(pl.BlockSpec((rows_q, 256), lambda b, i: (row(b, i), 2)))
    args.append(cd)
    if latent:
        in_specs += [
            pl.BlockSpec((HALO, 256), lambda b, i: (jnp.maximum(row(b, i) * row8 - 1, 0), 2)),
            pl.BlockSpec((HALO, 256), lambda b, i: (jnp.minimum((row(b, i) + 1) * row8, last8), 2)),
        ]
        args += [cd, cd]
    in_specs += [
        pl.BlockSpec((rows_q, 512), lambda b, i: (row(b, i), 0)),
        pl.BlockSpec((tq, 256), lambda b, i: (i, 0)),
        pl.BlockSpec((None, SUBLANES, LANES), lambda b, i: (layer, 0, 0)),
        pl.BlockSpec((None, 1, LANES), lambda b, i: (layer, 0, 0)),
        pl.BlockSpec((None, 256, 256), lambda b, i: (layer, 0, 0)),
        pl.BlockSpec((None, 1, 256), lambda b, i: (layer, 0, 0)),
        pl.BlockSpec((None, 1, 256), lambda b, i: (layer, 0, 0)),
        pl.BlockSpec((None, CHUNK, 4 * CHUNK), lambda b, i: (layer, 0, 0)),
        pl.BlockSpec((None, CHUNK, 256), lambda b, i: (layer, 0, 0)),
    ]
    args += [cd, icnt, lamp, subln, cw, cscale, gv, ws, bs]
    return pl.pallas_call(
        functools.partial(_mixer_kernel, latent=latent, lam_init=lam_init, tq=tq, n_q=n_q, n_sub=n_sub, seq=seq,
                          ahead=1 if latent else None),
        grid=(n_batch // n_sub, n_q),
        in_specs=in_specs,
        out_specs=pl.BlockSpec((rows_q, D_MODEL), lambda b, i: (row(b, i), 0)),
        out_shape=jax.ShapeDtypeStruct((n, D_MODEL), BF16),
        compiler_params=pltpu.CompilerParams(
            dimension_semantics=("arbitrary", "arbitrary"), vmem_limit_bytes=VMEM_LIMIT_BYTES),
        name="mixers_latent" if latent else "mixers_context",
    )(*args)


def _ffn_kernel(mix_ref, x_ref, mod_ref, wout_ref, gpm_ref, gpf_ref, gpo_ref, wg_ref, wu_ref, wd_ref,
                out_ref, acc_ref):
    y = _dot(mix_ref[...], wout_ref[...])
    x1 = x_ref[...] + mod_ref[2:3, :] * _rms(y, gpm_ref[...])
    h = (_rms(x1, gpf_ref[...]) * (1.0 + mod_ref[4:5, :]) + mod_ref[3:4, :]).astype(BF16)
    for c in range(N_FF_CHUNKS):
        cols = slice(c * FF_CHUNK, (c + 1) * FF_CHUNK)
        g = _dot(h, wg_ref[:, cols])
        u = _dot(h, wu_ref[:, cols])
        part = _dot((jax.nn.silu(g) * u).astype(BF16), wd_ref[cols, :])
        if c == 0:
            acc_ref[...] = part
        else:
            acc_ref[...] += part
    out_ref[...] = x1 + mod_ref[5:6, :] * _rms(acc_ref[...], gpo_ref[...])


def _outproj_ffn(mixed, x, mods, layer, wout, gpm, gpf, gpo, wg, wu, wd, *, latent, seq):
    n = x.shape[0]
    tm = FFN_TOKEN_TILE
    tiles_per_seq = seq // tm if latent else 1
    if latent:
        mod_row = lambda i: 1 + i // tiles_per_seq
    else:
        mod_row = lambda i: 0
    const = dict(pipeline_mode=pl.Buffered(1))
    gain = lambda: pl.BlockSpec((None, 1, D_MODEL), lambda i: (layer, 0, 0))
    return pl.pallas_call(
        _ffn_kernel,
        grid=(n // tm,),
        in_specs=[
            pl.BlockSpec((tm, D_MODEL), lambda i: (i, 0)),
            pl.BlockSpec((tm, D_MODEL), lambda i: (i, 0)),
            pl.BlockSpec((None, None, N_MOD, D_MODEL), lambda i: (layer, mod_row(i), 0, 0)),
            pl.BlockSpec((None, D_MODEL, D_MODEL), lambda i: (layer, 0, 0), **const),
            gain(), gain(), gain(),
            pl.BlockSpec((None, D_MODEL, D_FF), lambda i: (layer, 0, 0), **const),
            pl.BlockSpec((None, D_MODEL, D_FF), lambda i: (layer, 0, 0), **const),
            pl.BlockSpec((None, D_FF, D_MODEL), lambda i: (layer, 0, 0), **const),
        ],
        out_specs=pl.BlockSpec((tm, D_MODEL), lambda i: (i, 0)),
        out_shape=jax.ShapeDtypeStruct((n, D_MODEL), F32),
        scratch_shapes=[pltpu.VMEM((tm, D_MODEL), F32)],
        compiler_params=pltpu.CompilerParams(
            dimension_semantics=("arbitrary",), vmem_limit_bytes=VMEM_LIMIT_BYTES),
        name="outproj_ffn_latent" if latent else "outproj_ffn_context",
    )(mixed, x, mods, wout, gpm, gpf, gpo, wg, wu, wd)


def _rope_tables(seq):
    t = np.arange(seq)
    pos = {"row": (t // GRID_W).astype(np.float64), "col": (t % GRID_W).astype(np.float64)}

    def table(head_dim):
        half = head_dim // 2
        inv_freq = ROPE_THETA ** (-np.arange(0, half, 2, dtype=np.float64) / half)
        lane = np.arange(head_dim)
        within = lane % half
        freq = inv_freq[within % (half // 2)]
        p = np.where(lane[None, :] < half, pos["row"][:, None], pos["col"][:, None])
        ang = p * freq[None, :]
        sign = np.where(within < half // 2, -1.0, 1.0)
        reps = LANES // head_dim
        return (np.tile(np.cos(ang), (1, reps)).astype(np.float32),
                np.tile(np.sin(ang) * sign[None, :], (1, reps)).astype(np.float32))

    cosa, sina = table(A_HEAD_DIM)
    cosb, sinb = table(B_QK_DIM)
    return cosa, sina, cosb, sinb


def _inv_count_table(seq):
    t = np.arange(seq)
    cols = []
    for win in POOL_WINDOWS:
        cnt = np.minimum(t + win // 2, seq) - np.maximum(t - win // 2, 0)
        cols.append(np.repeat((1.0 / cnt)[:, None], 64, axis=1))
    return np.concatenate(cols, axis=1).astype(np.float32)


def _segment_mean_matrix():
    idx = np.arange(384) // A_HEAD_DIM
    return (idx[:, None] == idx[None, :]).astype(np.float32) / A_HEAD_DIM


def kernel(x_prompt, x_sample, cache_a_k, cache_a_v, cache_b_k, cache_b_v, c, c_ctx, w_mod, b_mod, g_pre_mix, g_post_mix, g_pre_ffn, g_post_ffn, w_in, w_out, a_q_norm, a_k_norm, b_lq1, b_lk1, b_lq2, b_lk2, b_subln, c_w, c_scale, d_v_norm, d_ws, d_bs, w_gate, w_up, w_down):
    batch, seq, _ = x_prompt.shape
    dec_batch, dec_seq, _ = x_sample.shape
    past = cache_a_k.shape[2]

    win = w_in.astype(BF16)
    wout = w_out.astype(BF16)
    wg = w_gate.astype(BF16)
    wu = w_up.astype(BF16)
    wd = w_down.astype(BF16)
    gqk = jnp.concatenate([jnp.tile(a_q_norm, (1, 4)), jnp.tile(a_k_norm, (1, 2))], axis=1).reshape(DEPTH, 1, 384)
    seg = jnp.asarray(_segment_mean_matrix(), BF16)
    lamp = jnp.stack([b_lq1, b_lk1, b_lq2, b_lk2], axis=1)
    lamp = jnp.pad(lamp, ((0, 0), (0, SUBLANES - 4), (0, LANES - B_QK_DIM)))
    subln = jnp.tile(b_subln, (1, 2)).reshape(DEPTH, 1, LANES)
    eye = jnp.eye(4, dtype=F32)
    cw = (eye[None, :, None, :, None] * c_w[:, :, :, None, :]).reshape(DEPTH, 256, 256).astype(BF16)
    cscale = c_scale.reshape(DEPTH, 1, GROUP_W)
    gv = d_v_norm.reshape(DEPTH, 1, GROUP_W)
    ws = d_ws.transpose(0, 2, 1, 3).reshape(DEPTH, CHUNK, 4 * CHUNK).astype(BF16)
    bs = jnp.repeat(d_bs.transpose(0, 2, 1), 64, axis=2)
    gpre = g_pre_mix.reshape(DEPTH, 1, D_MODEL)
    gpm = g_post_mix.reshape(DEPTH, 1, D_MODEL)
    gpf = g_pre_ffn.reshape(DEPTH, 1, D_MODEL)
    gpo = g_post_ffn.reshape(DEPTH, 1, D_MODEL)
    rope = tuple(jnp.asarray(t) for t in _rope_tables(dec_seq))
    icnt_ctx = jnp.asarray(_inv_count_table(seq))
    icnt_lat = jnp.asarray(_inv_count_table(dec_seq))
    flat = lambda a, w: a.reshape(dec_batch, DEPTH, past, w).astype(BF16)
    ctx = (flat(cache_a_k, 128),
           jnp.concatenate([flat(cache_a_v, 128), jnp.ones((dec_batch, DEPTH, past, 128), BF16)], axis=-1),
           flat(cache_b_k, 256), flat(cache_b_v, 256))

    cond8 = jnp.concatenate([c_ctx[None, :], c, jnp.zeros((SUBLANES - 1 - dec_batch, D_MODEL), F32)], axis=0)
    mods = _modulation(cond8, w_mod, b_mod)

    yp = x_prompt.reshape(batch * seq, D_MODEL)
    ys = x_sample.reshape(dec_batch * dec_seq, D_MODEL)
    own = [jnp.zeros((batch, DEPTH, seq, w), F32) for w in (128, 128, 256, 256)]
    for layer in range(DEPTH):
        lam_init = 0.8 - 0.6 * math.exp(-0.3 * layer)
        mix_args = (lamp, subln, cw, cscale, gv, ws, bs, layer)
        ffn_args = (mods, layer, wout, gpm, gpf, gpo, wg, wu, wd)
        qkv, cd, *own = _inproj(yp, mods, layer, gpre, win, gqk, seg, None, own, latent=False, seq=seq)
        mixed = _mixers(qkv, cd, None, icnt_ctx, *mix_args, latent=False, seq=seq, lam_init=lam_init)
        yp = _outproj_ffn(mixed, yp, *ffn_args, latent=False, seq=seq)
        qkv, cd = _inproj(ys, mods, layer, gpre, win, gqk, seg, rope, None, latent=True, seq=dec_seq)
        mixed = _mixers(qkv, cd, ctx, icnt_lat, *mix_args, latent=True, seq=dec_seq, lam_init=lam_init)
        ys = _outproj_ffn(mixed, ys, *ffn_args, latent=True, seq=dec_seq)

    new_a_k, new_a_v, new_b_k, new_b_v = own
    lead = (batch, DEPTH, seq)
    return (yp.reshape(batch, seq, D_MODEL), ys.reshape(dec_batch, dec_seq, D_MODEL),
            new_a_k.reshape(lead + (2, 64)), new_a_v.reshape(lead + (2, 64)),
            new_b_k.reshape(lead + (4, 2, 32)), new_b_v.reshape(lead + (4, 64)))
```

```python
import functools
import math

import numpy as np
import jax
import jax.numpy as jnp
from jax import lax
from jax.experimental import pallas as pl
from jax.experimental.pallas import tpu as pltpu

F32 = jnp.float32
BF16 = jnp.bfloat16

D_MODEL = 1024
DEPTH = 4
GRID_W = 64
GROUP_W = 256
A_HEAD_DIM = 64
B_QK_DIM = 32
POOL_WINDOWS = (2, 4, 8, 16)
CHUNK = 128
D_FF = 2816
N_MOD = 6
RMS_EPS = 1e-6
LOG2E = math.log2(math.e)
ROPE_THETA = 10000.0
IN_WIDTH = 2048

LANES = 128
SUBLANES = 8
VMEM_LIMIT_BYTES = 56 * 1024 * 1024

COL_AQ, COL_AK, COL_AV, COL_BQ, COL_BK, COL_BV, COL_CX, COL_DZ = 0, 256, 384, 512, 768, 1024, 1280, 1536
QKV_AQ, QKV_AV, QKV_BQ, QKV_BK, QKV_BV, QKV_AK = 0, 256, 512, 768, 1024, 1280
QKV_WIDTH = 1408
CD_WIDTH = 768
OWN_WIDTHS = (128, 128, 256, 256)
FF_CHUNK = 256
N_FF_CHUNKS = D_FF // FF_CHUNK

TOKEN_TILE = 512
FFN_TOKEN_TILE = 512
Q_TILE = 512
CONTEXT_SEQS_PER_STEP = 4
MOD_COL_TILE = 1536
HALO = 8

def _rms(x, g):
    ms = jnp.mean(x * x, axis=-1, keepdims=True)
    return x * lax.rsqrt(ms + RMS_EPS) * g


def _dot(a, b):
    return jnp.dot(a, b, preferred_element_type=F32)


def _dot_nt(a, b):
    return lax.dot_general(a, b, (((1,), (1,)), ((), ())), preferred_element_type=F32)


def _pair_heads(c0, c1, half0):
    return jnp.where(half0, c0, c1), pltpu.roll(jnp.where(half0, c1, c0), 64, axis=1)


def _unpair_heads(d0, d1, half0):
    r = pltpu.roll(d1, 64, axis=1)
    return jnp.where(half0, d0, r), jnp.where(half0, r, d0)


def _mod_kernel(cond_ref, w_ref, b_ref, out_ref):
    s = jax.nn.silu(cond_ref[...]).astype(BF16)
    out_ref[...] = _dot(s, w_ref[...].astype(BF16)) + b_ref[...]


def _modulation(cond8, w_mod, b_mod):
    n_col = (N_MOD * D_MODEL) // MOD_COL_TILE
    out = pl.pallas_call(
        _mod_kernel,
        grid=(DEPTH, n_col),
        in_specs=[
            pl.BlockSpec((SUBLANES, D_MODEL), lambda l, j: (0, 0)),
            pl.BlockSpec((None, D_MODEL, MOD_COL_TILE), lambda l, j: (l, 0, j)),
            pl.BlockSpec((None, 1, MOD_COL_TILE), lambda l, j: (l, 0, j)),
        ],
        out_specs=pl.BlockSpec((None, SUBLANES, MOD_COL_TILE), lambda l, j: (l, 0, j)),
        out_shape=jax.ShapeDtypeStruct((DEPTH, SUBLANES, N_MOD * D_MODEL), F32),
        compiler_params=pltpu.CompilerParams(
            dimension_semantics=("arbitrary", "arbitrary"), vmem_limit_bytes=VMEM_LIMIT_BYTES),
        name="modulation",
    )(cond8, w_mod, b_mod.reshape(DEPTH, 1, N_MOD * D_MODEL))
    return out.reshape(DEPTH, SUBLANES, N_MOD, D_MODEL)


def _rope_cols(x, cos, sin, shift, lo_mask):
    fwd = pltpu.roll(x, LANES - shift, axis=1)
    bwd = pltpu.roll(x, shift, axis=1)
    return x * cos + jnp.where(lo_mask, fwd, bwd) * sin


def _inproj_kernel(*refs, latent, n_aliased):
    if latent:
        (x_ref, mod_ref, gpre_ref, win_ref, gqk_ref, seg_ref, cosa_ref, sina_ref, cosb_ref, sinb_ref,
         qkv_ref, cd_ref) = refs
    else:
        (x_ref, mod_ref, gpre_ref, win_ref, gqk_ref, seg_ref) = refs[:6]
        qkv_ref, cd_ref, oak_ref, oav_ref, obk_ref, obv_ref = refs[6 + n_aliased:]
    h = _rms(x_ref[...], gpre_ref[...]) * (1.0 + mod_ref[1:2, :]) + mod_ref[0:1, :]
    p = _dot(h.astype(BF16), win_ref[...])

    pa = p[:, COL_AQ:COL_AV]
    msq = _dot((pa * pa).astype(BF16), seg_ref[...])
    pa = pa * lax.rsqrt(msq + RMS_EPS) * gqk_ref[...]
    pb = p[:, COL_BQ:COL_BV]
    lane = lax.broadcasted_iota(jnp.int32, (1, LANES), 1)
    if not latent:
        oak_ref[...] = pa[:, COL_AK:COL_AV].reshape(oak_ref.shape)
        oav_ref[...] = p[:, COL_AV:COL_BQ].reshape(oav_ref.shape)
        obk_ref[...] = p[:, COL_BK:COL_BV].reshape(obk_ref.shape)
        obv_ref[...] = p[:, COL_BV:COL_CX].reshape(obv_ref.shape)
        a_cols = [pa[:, c * LANES:(c + 1) * LANES] for c in range(3)]
        b_cols = [pb[:, c * LANES:(c + 1) * LANES] for c in range(4)]
    else:
        lo_a = (lane % 32) < 16
        lo_b = (lane % 16) < 8
        cosa, sina, cosb, sinb = cosa_ref[...], sina_ref[...], cosb_ref[...], sinb_ref[...]
        a_cols = [_rope_cols(pa[:, c * LANES:(c + 1) * LANES], cosa, sina, 16, lo_a) for c in range(3)]
        b_cols = [_rope_cols(pb[:, c * LANES:(c + 1) * LANES], cosb, sinb, 8, lo_b) for c in range(4)]
    a_scale = A_HEAD_DIM ** -0.5 * LOG2E
    b_scale = B_QK_DIM ** -0.5 * LOG2E
    aq0, aq1 = _pair_heads(a_cols[0] * a_scale, a_cols[1] * a_scale, lane < 64)
    qkv_ref[:, QKV_AQ:QKV_AQ + 128] = aq0.astype(BF16)
    qkv_ref[:, QKV_AQ + 128:QKV_AQ + 256] = aq1.astype(BF16)
    qkv_ref[:, QKV_AV:QKV_AV + 128] = p[:, COL_AV:COL_BQ].astype(BF16)
    qkv_ref[:, QKV_AV + 128:QKV_AV + 256] = jnp.ones((p.shape[0], LANES), BF16)
    qkv_ref[:, QKV_BQ:QKV_BQ + 128] = (b_cols[0] * b_scale).astype(BF16)
    qkv_ref[:, QKV_BQ + 128:QKV_BQ + 256] = (b_cols[1] * b_scale).astype(BF16)
    qkv_ref[:, QKV_BK:QKV_BK + 128] = b_cols[2].astype(BF16)
    qkv_ref[:, QKV_BK + 128:QKV_BK + 256] = b_cols[3].astype(BF16)
    qkv_ref[:, QKV_BV:QKV_BV + 256] = p[:, COL_BV:COL_CX].astype(BF16)
    qkv_ref[:, QKV_AK:QKV_AK + 128] = a_cols[2].astype(BF16)
    cd_ref[:, 0:512] = p[:, COL_DZ:IN_WIDTH]
    cd_ref[:, 512:768] = p[:, COL_CX:COL_DZ]


def _inproj(x, mods, layer, gpre, win, gqk, seg, rope, own, *, latent, seq):
    n = x.shape[0]
    tm = TOKEN_TILE
    tiles_per_seq = seq // tm if latent else 1
    if latent:
        mod_row = lambda i: 1 + i // tiles_per_seq
    else:
        mod_row = lambda i: 0
    in_specs = [
        pl.BlockSpec((tm, D_MODEL), lambda i: (i, 0)),
        pl.BlockSpec((None, None, N_MOD, D_MODEL), lambda i: (layer, mod_row(i), 0, 0)),
        pl.BlockSpec((None, 1, D_MODEL), lambda i: (layer, 0, 0)),
        pl.BlockSpec((None, D_MODEL, IN_WIDTH), lambda i: (layer, 0, 0)),
        pl.BlockSpec((None, 1, 384), lambda i: (layer, 0, 0)),
        pl.BlockSpec((384, 384), lambda i: (0, 0)),
    ]
    args = [x, mods, gpre, win, gqk, seg]
    out_specs = [pl.BlockSpec((tm, QKV_WIDTH), lambda i: (i, 0)),
                 pl.BlockSpec((tm, CD_WIDTH), lambda i: (i, 0))]
    out_shape = [jax.ShapeDtypeStruct((n, QKV_WIDTH), BF16), jax.ShapeDtypeStruct((n, CD_WIDTH), F32)]
    aliases = {}
    if latent:
        in_specs += [pl.BlockSpec((tm, LANES), lambda i: (i % tiles_per_seq, 0))] * 4
        args += list(rope)
    else:
        for arr, width in zip(own, OWN_WIDTHS):
            aliases[len(args)] = len(out_shape)
            in_specs.append(pl.BlockSpec(memory_space=pl.ANY))
            args.append(arr)
            out_specs.append(pl.BlockSpec((tm // seq, None, seq, width), lambda i: (i, layer, 0, 0)))
            out_shape.append(jax.ShapeDtypeStruct(arr.shape, arr.dtype))
    return pl.pallas_call(
        functools.partial(_inproj_kernel, latent=latent, n_aliased=len(aliases)),
        grid=(n // tm,),
        in_specs=in_specs,
        out_specs=out_specs,
        out_shape=out_shape,
        input_output_aliases=aliases,
        compiler_params=pltpu.CompilerParams(
            dimension_semantics=("arbitrary",), vmem_limit_bytes=VMEM_LIMIT_BYTES),
        name="inproj_latent" if latent else "inproj_context",
    )(*args)


def _scores(qm, k_segs):
    return [_dot_nt(qm, k) for k in k_segs]


def _numerators(ss):
    m = functools.reduce(jnp.maximum, [jnp.max(s, axis=-1, keepdims=True) for s in ss])
    return [jnp.exp2(s - m) for s in ss]


def _rowsum(ps):
    return functools.reduce(jnp.add, [jnp.sum(p, axis=-1, keepdims=True) for p in ps])


def _pool_windows(xh, tq):
    n = tq + 2 * HALO
    up = lambda a, k: pltpu.roll(a, n - k, axis=0)
    a2 = xh + up(xh, 1)
    a4 = a2 + up(a2, 2)
    a8 = a4 + up(a4, 4)
    a16 = a8 + up(a8, 8)
    w2 = up(a2, HALO - 1)[0:tq]
    w4 = up(a4, HALO - 2)[0:tq]
    w8 = up(a8, HALO - 4)[0:tq]
    w16 = a16[0:tq]
    return w2, w4, w8, w16


def _mixer_kernel(*refs, latent, lam_init, tq, n_q, n_sub, seq, ahead):
    if latent:
        (aq_ref, ak_ref, av_ref, bq_ref, bk_ref, bv_ref, cak_ref, cav_ref, cbk_ref, cbv_ref,
         cx_ref, cxp_ref, cxn_ref, dz_ref, icnt_ref, lamp_ref, subln_ref, cw_ref, cscale_ref,
         gv_ref, ws_ref, bs_ref, out_ref) = refs
    else:
        (aq_ref, ak_ref, av_ref, bq_ref, bk_ref, bv_ref,
         cx_ref, dz_ref, icnt_ref, lamp_ref, subln_ref, cw_ref, cscale_ref,
         gv_ref, ws_ref, bs_ref, out_ref) = refs
        cak_ref = cav_ref = cbk_ref = cbv_ref = cxp_ref = cxn_ref = None
    i = pl.program_id(1)
    lane = lax.broadcasted_iota(jnp.int32, (1, LANES), 1)
    half0 = lane < 64

    def pool_mixer(r):
        rows = slice(r * tq, (r + 1) * tq)
        cur = cx_ref[rows, :]
        if latent:
            prev = jnp.where(i == 0, 0.0, cxp_ref[...])
            nxt = jnp.where(i == n_q - 1, 0.0, cxn_ref[...])
        else:
            prev = nxt = jnp.zeros((HALO, GROUP_W), F32)
        xh = jnp.concatenate([prev, cur, nxt], axis=0)
        w2, w4, _, _ = _pool_windows(xh[:, 0:LANES], tq)
        _, _, w8, w16 = _pool_windows(xh[:, LANES:2 * LANES], tq)
        sums = jnp.concatenate([jnp.where(half0, w2, w4), jnp.where(half0, w8, w16)], axis=1)
        pooled = sums * icnt_ref[...] - cur
        yc = _dot(pooled.astype(BF16), cw_ref[...]) * cscale_ref[...]
        out_ref[rows, 512:768] = yc.astype(BF16)

    lane2 = lax.broadcasted_iota(jnp.int32, (1, GROUP_W), 1)

    def gate_mixer(n):
        rows = slice(n * CHUNK, (n + 1) * CHUNK)
        z = jax.nn.gelu(dz_ref[rows, :])
        vc = _rms(z[:, GROUP_W:2 * GROUP_W], gv_ref[...])
        stacked = jnp.concatenate(
            [jnp.where((lane2 >= 64 * g) & (lane2 < 64 * (g + 1)), vc, 0.0).astype(BF16) for g in range(4)],
            axis=0)
        sg = _dot(ws_ref[...], stacked) + bs_ref[...]
        out_ref[rows, 768:1024] = (z[:, 0:GROUP_W] * sg).astype(BF16)

    lp = lamp_ref[...]
    lam = (jnp.exp(jnp.sum(lp[0:1] * lp[1:2], axis=-1, keepdims=True))
           - jnp.exp(jnp.sum(lp[2:3] * lp[3:4], axis=-1, keepdims=True)) + lam_init)

    def key_segments(ctx_ref, own_ref, r, cols):
        return ([ctx_ref[:, cols]] if latent else []) + [own_ref[r * seq:(r + 1) * seq, cols]]

    def head_queries(t):
        r, kind, c, u = heads[t]
        q_ref = aq_ref if kind == "a" else bq_ref
        qc = q_ref[r * tq:(r + 1) * tq, c * LANES:(c + 1) * LANES]
        zero = jnp.zeros_like(qc)
        if kind == "a":
            return [jnp.where(half0 if u == 0 else ~half0, qc, zero)]
        sels = [(lane >= 64 * u + 32 * j) & (lane < 64 * u + 32 * j + 32) for j in range(2)]
        return [jnp.where(sel, qc, zero) for sel in sels]

    def head_scores(t):
        r, kind, c, _ = heads[t]
        if kind == "a":
            k_segs = key_segments(cak_ref, ak_ref, r, slice(0, LANES))
        else:
            k_segs = key_segments(cbk_ref, bk_ref, r, slice(c * LANES, (c + 1) * LANES))
        return [_scores(qm, k_segs) for qm in head_queries(t)]

    def head_weights(t, score_sets):
        kind = heads[t][1]
        if kind == "a":
            return [p.astype(BF16) for p in _numerators(score_sets[0])], None
        p0, p1 = _numerators(score_sets[0]), _numerators(score_sets[1])
        l0 = _rowsum(p0)
        ratio = lam * l0 / _rowsum(p1)
        return [(a - b * ratio).astype(BF16) for a, b in zip(p0, p1)], l0

    def head_output(t, weights):
        r, kind, c, u = heads[t]
        ws, l0 = weights
        if kind == "a":
            a_v = key_segments(cav_ref, av_ref, r, slice(0, 2 * LANES))
            o = functools.reduce(jnp.add, [_dot(w, v) for w, v in zip(ws, a_v)])
            return o[:, 0:LANES] / o[:, LANES:2 * LANES]
        b_v = key_segments(cbv_ref, bv_ref, r, slice(0, 2 * LANES))
        o = functools.reduce(jnp.add, [_dot(w, v) for w, v in zip(ws, b_v)])
        o = o[:, c * LANES:(c + 1) * LANES] / l0
        valid = half0 if u == 0 else ~half0
        ms = jnp.sum(jnp.where(valid, o * o, 0.0), axis=-1, keepdims=True) * (1.0 / 64)
        return o * lax.rsqrt(ms + RMS_EPS)

    heads = [(r, kind, c, u) for r in range(n_sub) for c in range(2) for u in range(2) for kind in ("a", "b")]
    side_work = ([functools.partial(pool_mixer, r) for r in range(n_sub)]
                 + [functools.partial(gate_mixer, n) for n in range(n_sub * tq // CHUNK)])
    outs = {}
    if ahead is None:
        scores = [head_scores(t) for t in range(len(heads))]
        for work in side_work:
            work()
        weights = [head_weights(t, scores[t]) for t in range(len(heads))]
        for t in range(len(heads)):
            outs[heads[t]] = head_output(t, weights[t])
    else:
        pending = [head_scores(t) for t in range(min(ahead, len(heads)))]
        for t in range(len(heads)):
            if t + ahead < len(heads):
                pending.append(head_scores(t + ahead))
            if side_work:
                side_work.pop(0)()
            outs[heads[t]] = head_output(t, head_weights(t, pending.pop(0)))
        for work in side_work:
            work()
    b_gain = subln_ref[...] * (1.0 - lam_init)
    for r in range(n_sub):
        rows = slice(r * tq, (r + 1) * tq)
        col = lambda kind, c: jnp.where(half0, outs[(r, kind, c, 0)], outs[(r, kind, c, 1)])
        a0, a1 = _unpair_heads(col("a", 0), col("a", 1), half0)
        out_ref[rows, 0:128] = a0.astype(BF16)
        out_ref[rows, 128:256] = a1.astype(BF16)
        out_ref[rows, 256:384] = (col("b", 0) * b_gain).astype(BF16)
        out_ref[rows, 384:512] = (col("b", 1) * b_gain).astype(BF16)


def _mixers(qkv, cd, ctx, icnt, lamp, subln, cw, cscale, gv, ws, bs, layer, *, latent, seq, lam_init):
    n = qkv.shape[0]
    n_batch = n // seq
    tq = min(Q_TILE, seq)
    n_q = seq // tq
    n_sub = 1 if latent else CONTEXT_SEQS_PER_STEP
    rows_q = n_sub * tq
    rows_k = n_sub * seq
    row = lambda b, i: b * n_q + i
    row8 = tq // SUBLANES
    last8 = n // SUBLANES - 1
    in_specs = [
        pl.BlockSpec((rows_q, 256), lambda b, i: (row(b, i), QKV_AQ // 256)),
        pl.BlockSpec((rows_k, 128), lambda b, i: (b, QKV_AK // 128)),
        pl.BlockSpec((rows_k, 256), lambda b, i: (b, QKV_AV // 256)),
        pl.BlockSpec((rows_q, 256), lambda b, i: (row(b, i), QKV_BQ // 256)),
        pl.BlockSpec((rows_k, 256), lambda b, i: (b, QKV_BK // 256)),
        pl.BlockSpec((rows_k, 256), lambda b, i: (b, QKV_BV // 256)),
    ]
    args = [qkv] * 6
    if latent:
        for w in (128, 256, 256, 256):
            in_specs.append(pl.BlockSpec((None, None, 256, w), lambda b, i: (b, layer, 0, 0)))
        args += list(ctx)
    in_specs.append(pl.BlockSpec((rows_q, 256), lambda b, i: (row(b, i), 2)))
    args.append(cd)
    if latent:
        in_specs += [
            pl.BlockSpec((HALO, 256), lambda b, i: (jnp.maximum(row(b, i) * row8 - 1, 0), 2)),
            pl.BlockSpec((HALO, 256), lambda b, i: (jnp.minimum((row(b, i) + 1) * row8, last8), 2)),
        ]
        args += [cd, cd]
    in_specs += [
        pl.BlockSpec((rows_q, 512), lambda b, i: (row(b, i), 0)),
        pl.BlockSpec((tq, 256), lambda b, i: (i, 0)),
        pl.BlockSpec((None, SUBLANES, LANES), lambda b, i: (layer, 0, 0)),
        pl.BlockSpec((None, 1, LANES), lambda b, i: (layer, 0, 0)),
        pl.BlockSpec((None, 256, 256), lambda b, i: (layer, 0, 0)),
        pl.BlockSpec((None, 1, 256), lambda b, i: (layer, 0, 0)),
        pl.BlockSpec((None, 1, 256), lambda b, i: (layer, 0, 0)),
        pl.BlockSpec((None, CHUNK, 4 * CHUNK), lambda b, i: (layer, 0, 0)),
        pl.BlockSpec((None, CHUNK, 256), lambda b, i: (layer, 0, 0)),
    ]
    args += [cd, icnt, lamp, subln, cw, cscale, gv, ws, bs]
    return pl.pallas_call(
        functools.partial(_mixer_kernel, latent=latent, lam_init=lam_init, tq=tq, n_q=n_q, n_sub=n_sub, seq=seq,
                          ahead=1 if latent else None),
        grid=(n_batch // n_sub, n_q),
        in_specs=in_specs,
        out_specs=pl.BlockSpec((rows_q, D_MODEL), lambda b, i: (row(b, i), 0)),
        out_shape=jax.ShapeDtypeStruct((n, D_MODEL), BF16),
        compiler_params=pltpu.CompilerParams(
            dimension_semantics=("arbitrary", "arbitrary"), vmem_limit_bytes=VMEM_LIMIT_BYTES),
        name="mixers_latent" if latent else "mixers_context",
    )(*args)


def _ffn_kernel(mix_ref, x_ref, mod_ref, wout_ref, gpm_ref, gpf_ref, gpo_ref, wg_ref, wu_ref, wd_ref,
                out_ref, acc_ref):
    tm = x_ref.shape[0]
    halves = [slice(0, tm // 2), slice(tm // 2, tm)]

    def ffn_chunk(h, c):
        cols = slice(c * FF_CHUNK, (c + 1) * FF_CHUNK)
        g = _dot(h, wg_ref[:, cols])
        u = _dot(h, wu_ref[:, cols])
        return _dot((jax.nn.silu(g) * u).astype(BF16), wd_ref[cols, :])

    x1s, hs = [], []
    for rows in halves:
        y = _dot(mix_ref[rows, :], wout_ref[...])
        x1 = x_ref[rows, :] + mod_ref[2:3, :] * _rms(y, gpm_ref[...])
        x1s.append(x1)
        hs.append((_rms(x1, gpf_ref[...]) * (1.0 + mod_ref[4:5, :]) + mod_ref[3:4, :]).astype(BF16))
    for rows, h in zip(halves, hs):
        acc_ref[rows, :] = ffn_chunk(h, 0)
    h_all = jnp.concatenate(hs, axis=0)
    for c in range(1, N_FF_CHUNKS - 1):
        acc_ref[...] += ffn_chunk(h_all, c)
    for rows, h, x1 in zip(halves, hs, x1s):
        f = acc_ref[rows, :] + ffn_chunk(h, N_FF_CHUNKS - 1)
        out_ref[rows, :] = x1 + mod_ref[5:6, :] * _rms(f, gpo_ref[...])


def _outproj_ffn(mixed, x, mods, layer, wout, gpm, gpf, gpo, wg, wu, wd, *, latent, seq):
    n = x.shape[0]
    tm = FFN_TOKEN_TILE
    tiles_per_seq = seq // tm if latent else 1
    if latent:
        mod_row = lambda i: 1 + i // tiles_per_seq
    else:
        mod_row = lambda i: 0
    const = dict(pipeline_mode=pl.Buffered(1))
    gain = lambda: pl.BlockSpec((None, 1, D_MODEL), lambda i: (layer, 0, 0))
    return pl.pallas_call(
        _ffn_kernel,
        grid=(n // tm,),
        in_specs=[
            pl.BlockSpec((tm, D_MODEL), lambda i: (i, 0)),
            pl.BlockSpec((tm, D_MODEL), lambda i: (i, 0)),
            pl.BlockSpec((None, None, N_MOD, D_MODEL), lambda i: (layer, mod_row(i), 0, 0)),
            pl.BlockSpec((None, D_MODEL, D_MODEL), lambda i: (layer, 0, 0), **const),
            gain(), gain(), gain(),
            pl.BlockSpec((None, D_MODEL, D_FF), lambda i: (layer, 0, 0), **const),
            pl.BlockSpec((None, D_MODEL, D_FF), lambda i: (layer, 0, 0), **const),
            pl.BlockSpec((None, D_FF, D_MODEL), lambda i: (layer, 0, 0), **const),
        ],
        out_specs=pl.BlockSpec((tm, D_MODEL), lambda i: (i, 0)),
        out_shape=jax.ShapeDtypeStruct((n, D_MODEL), F32),
        scratch_shapes=[pltpu.VMEM((tm, D_MODEL), F32)],
        compiler_params=pltpu.CompilerParams(
            dimension_semantics=("arbitrary",), vmem_limit_bytes=VMEM_LIMIT_BYTES),
        name="outproj_ffn_latent" if latent else "outproj_ffn_context",
    )(mixed, x, mods, wout, gpm, gpf, gpo, wg, wu, wd)


def _rope_tables(seq):
    t = np.arange(seq)
    pos = {"row": (t // GRID_W).astype(np.float64), "col": (t % GRID_W).astype(np.float64)}

    def table(head_dim):
        half = head_dim // 2
        inv_freq = ROPE_THETA ** (-np.arange(0, half, 2, dtype=np.float64) / half)
        lane = np.arange(head_dim)
        within = lane % half
        freq = inv_freq[within % (half // 2)]
        p = np.where(lane[None, :] < half, pos["row"][:, None], pos["col"][:, None])
        ang = p * freq[None, :]
        sign = np.where(within < half // 2, -1.0, 1.0)
        reps = LANES // head_dim
        return (np.tile(np.cos(ang), (1, reps)).astype(np.float32),
                np.tile(np.sin(ang) * sign[None, :], (1, reps)).astype(np.float32))

    cosa, sina = table(A_HEAD_DIM)
    cosb, sinb = table(B_QK_DIM)
    return cosa, sina, cosb, sinb


def _inv_count_table(seq):
    t = np.arange(seq)
    cols = []
    for win in POOL_WINDOWS:
        cnt = np.minimum(t + win // 2, seq) - np.maximum(t - win // 2, 0)
        cols.append(np.repeat((1.0 / cnt)[:, None], 64, axis=1))
    return np.concatenate(cols, axis=1).astype(np.float32)


def _segment_mean_matrix():
    idx = np.arange(384) // A_HEAD_DIM
    return (idx[:, None] == idx[None, :]).astype(np.float32) / A_HEAD_DIM


def kernel(x_prompt, x_sample, cache_a_k, cache_a_v, cache_b_k, cache_b_v, c, c_ctx, w_mod, b_mod, g_pre_mix, g_post_mix, g_pre_ffn, g_post_ffn, w_in, w_out, a_q_norm, a_k_norm, b_lq1, b_lk1, b_lq2, b_lk2, b_subln, c_w, c_scale, d_v_norm, d_ws, d_bs, w_gate, w_up, w_down):
    batch, seq, _ = x_prompt.shape
    dec_batch, dec_seq, _ = x_sample.shape
    past = cache_a_k.shape[2]

    win = w_in.astype(BF16)
    wout = w_out.astype(BF16)
    wg = w_gate.astype(BF16)
    wu = w_up.astype(BF16)
    wd = w_down.astype(BF16)
    gqk = jnp.concatenate([jnp.tile(a_q_norm, (1, 4)), jnp.tile(a_k_norm, (1, 2))], axis=1).reshape(DEPTH, 1, 384)
    seg = jnp.asarray(_segment_mean_matrix(), BF16)
    lamp = jnp.stack([b_lq1, b_lk1, b_lq2, b_lk2], axis=1)
    lamp = jnp.pad(lamp, ((0, 0), (0, SUBLANES - 4), (0, LANES - B_QK_DIM)))
    subln = jnp.tile(b_subln, (1, 2)).reshape(DEPTH, 1, LANES)
    eye = jnp.eye(4, dtype=F32)
    cw = (eye[None, :, None, :, None] * c_w[:, :, :, None, :]).reshape(DEPTH, 256, 256).astype(BF16)
    cscale = c_scale.reshape(DEPTH, 1, GROUP_W)
    gv = d_v_norm.reshape(DEPTH, 1, GROUP_W)
    ws = d_ws.transpose(0, 2, 1, 3).reshape(DEPTH, CHUNK, 4 * CHUNK).astype(BF16)
    bs = jnp.repeat(d_bs.transpose(0, 2, 1), 64, axis=2)
    gpre = g_pre_mix.reshape(DEPTH, 1, D_MODEL)
    gpm = g_post_mix.reshape(DEPTH, 1, D_MODEL)
    gpf = g_pre_ffn.reshape(DEPTH, 1, D_MODEL)
    gpo = g_post_ffn.reshape(DEPTH, 1, D_MODEL)
    rope = tuple(jnp.asarray(t) for t in _rope_tables(dec_seq))
    icnt_ctx = jnp.asarray(_inv_count_table(seq))
    icnt_lat = jnp.asarray(_inv_count_table(dec_seq))
    flat = lambda a, w: a.reshape(dec_batch, DEPTH, past, w).astype(BF16)
    ctx = (flat(cache_a_k, 128),
           jnp.concatenate([flat(cache_a_v, 128), jnp.ones((dec_batch, DEPTH, past, 128), BF16)], axis=-1),
           flat(cache_b_k, 256), flat(cache_b_v, 256))

    cond8 = jnp.concatenate([c_ctx[None, :], c, jnp.zeros((SUBLANES - 1 - dec_batch, D_MODEL), F32)], axis=0)
    mods = _modulation(cond8, w_mod, b_mod)

    yp = x_prompt.reshape(batch * seq, D_MODEL)
    ys = x_sample.reshape(dec_batch * dec_seq, D_MODEL)
    own = [jnp.zeros((batch, DEPTH, seq, w), F32) for w in OWN_WIDTHS]
    for layer in range(DEPTH):
        lam_init = 0.8 - 0.6 * math.exp(-0.3 * layer)
        mix_args = (lamp, subln, cw, cscale, gv, ws, bs, layer)
        ffn_args = (mods, layer, wout, gpm, gpf, gpo, wg, wu, wd)
        qkv, cd, *own = _inproj(yp, mods, layer, gpre, win, gqk, seg, None, own, latent=False, seq=seq)
        mixed = _mixers(qkv, cd, None, icnt_ctx, *mix_args, latent=False, seq=seq, lam_init=lam_init)
        yp = _outproj_ffn(mixed, yp, *ffn_args, latent=False, seq=seq)
        qkv, cd = _inproj(ys, mods, layer, gpre, win, gqk, seg, rope, None, latent=True, seq=dec_seq)
        mixed = _mixers(qkv, cd, ctx, icnt_lat, *mix_args, latent=True, seq=dec_seq, lam_init=lam_init)
        ys = _outproj_ffn(mixed, ys, *ffn_args, latent=True, seq=dec_seq)

    new_a_k, new_a_v, new_b_k, new_b_v = own
    lead = (batch, DEPTH, seq)
    return (yp.reshape(batch, seq, D_MODEL), ys.reshape(dec_batch, dec_seq, D_MODEL),
            new_a_k.reshape(lead + (2, 64)), new_a_v.reshape(lead + (2, 64)),
            new_b_k.reshape(lead + (4, 2, 32)), new_b_v.reshape(lead + (4, 64)))
```

```python
import functools
import math

import numpy as np
import jax
import jax.numpy as jnp
from jax import lax
from jax.experimental import pallas as pl
from jax.experimental.pallas import tpu as pltpu

F32 = jnp.float32
BF16 = jnp.bfloat16

D_MODEL = 1024
DEPTH = 4
GRID_W = 64
GROUP_W = 256
A_HEAD_DIM = 64
B_QK_DIM = 32
POOL_WINDOWS = (2, 4, 8, 16)
CHUNK = 128
D_FF = 2816
N_MOD = 6
RMS_EPS = 1e-6
LOG2E = math.log2(math.e)
ROPE_THETA = 10000.0
IN_WIDTH = 2048

LANES = 128
SUBLANES = 8
VMEM_LIMIT_BYTES = 56 * 1024 * 1024

COL_AQ, COL_AK, COL_AV, COL_BQ, COL_BK, COL_BV, COL_CX, COL_DZ = 0, 256, 384, 512, 768, 1024, 1280, 1536
QKV_AQ, QKV_AV, QKV_BQ, QKV_BK, QKV_BV, QKV_AK = 0, 256, 512, 768, 1024, 1280
QKV_WIDTH = 1408
CD_WIDTH = 768
OWN_WIDTHS = (128, 128, 256, 256)
FF_CHUNK = 256
N_FF_CHUNKS = D_FF // FF_CHUNK

TOKEN_TILE = 1024
FFN_TOKEN_TILE = 1024
Q_TILE = 512
CONTEXT_SEQS_PER_STEP = 4
MOD_COL_TILE = 1536
HALO = 8

def _rms(x, g):
    ms = jnp.mean(x * x, axis=-1, keepdims=True)
    return x * lax.rsqrt(ms + RMS_EPS) * g


def _dot(a, b):
    return jnp.dot(a, b, preferred_element_type=F32)


def _dot_nt(a, b):
    return lax.dot_general(a, b, (((1,), (1,)), ((), ())), preferred_element_type=F32)


def _pair_heads(c0, c1, half0):
    return jnp.where(half0, c0, c1), pltpu.roll(jnp.where(half0, c1, c0), 64, axis=1)


def _unpair_heads(d0, d1, half0):
    r = pltpu.roll(d1, 64, axis=1)
    return jnp.where(half0, d0, r), jnp.where(half0, r, d0)


def _mod_kernel(cond_ref, w_ref, b_ref, out_ref):
    s = jax.nn.silu(cond_ref[...]).astype(BF16)
    out_ref[...] = _dot(s, w_ref[...].astype(BF16)) + b_ref[...]


def _modulation(cond8, w_mod, b_mod):
    n_col = (N_MOD * D_MODEL) // MOD_COL_TILE
    out = pl.pallas_call(
        _mod_kernel,
        grid=(DEPTH, n_col),
        in_specs=[
            pl.BlockSpec((SUBLANES, D_MODEL), lambda l, j: (0, 0)),
            pl.BlockSpec((None, D_MODEL, MOD_COL_TILE), lambda l, j: (l, 0, j)),
            pl.BlockSpec((None, 1, MOD_COL_TILE), lambda l, j: (l, 0, j)),
        ],
        out_specs=pl.BlockSpec((None, SUBLANES, MOD_COL_TILE), lambda l, j: (l, 0, j)),
        out_shape=jax.ShapeDtypeStruct((DEPTH, SUBLANES, N_MOD * D_MODEL), F32),
        compiler_params=pltpu.CompilerParams(
            dimension_semantics=("arbitrary", "arbitrary"), vmem_limit_bytes=VMEM_LIMIT_BYTES),
        name="modulation",
    )(cond8, w_mod, b_mod.reshape(DEPTH, 1, N_MOD * D_MODEL))
    return out.reshape(DEPTH, SUBLANES, N_MOD, D_MODEL)


def _rope_cols(x, cos, sin, shift, lo_mask):
    fwd = pltpu.roll(x, LANES - shift, axis=1)
    bwd = pltpu.roll(x, shift, axis=1)
    return x * cos + jnp.where(lo_mask, fwd, bwd) * sin


def _inproj_kernel(*refs, latent, n_aliased):
    if latent:
        (x_ref, mod_ref, gpre_ref, win_ref, gqk_ref, seg_ref, cosa_ref, sina_ref, cosb_ref, sinb_ref,
         qkv_ref, cd_ref) = refs
    else:
        (x_ref, mod_ref, gpre_ref, win_ref, gqk_ref, seg_ref) = refs[:6]
        qkv_ref, cd_ref, oak_ref, oav_ref, obk_ref, obv_ref = refs[6 + n_aliased:]
    h = _rms(x_ref[...], gpre_ref[...]) * (1.0 + mod_ref[1:2, :]) + mod_ref[0:1, :]
    p = _dot(h.astype(BF16), win_ref[...])

    pa = p[:, COL_AQ:COL_AV]
    msq = _dot((pa * pa).astype(BF16), seg_ref[...])
    pa = pa * lax.rsqrt(msq + RMS_EPS) * gqk_ref[...]
    pb = p[:, COL_BQ:COL_BV]
    lane = lax.broadcasted_iota(jnp.int32, (1, LANES), 1)
    if not latent:
        oak_ref[...] = pa[:, COL_AK:COL_AV].reshape(oak_ref.shape)
        oav_ref[...] = p[:, COL_AV:COL_BQ].reshape(oav_ref.shape)
        obk_ref[...] = p[:, COL_BK:COL_BV].reshape(obk_ref.shape)
        obv_ref[...] = p[:, COL_BV:COL_CX].reshape(obv_ref.shape)
        a_cols = [pa[:, c * LANES:(c + 1) * LANES] for c in range(3)]
        b_cols = [pb[:, c * LANES:(c + 1) * LANES] for c in range(4)]
    else:
        lo_a = (lane % 32) < 16
        lo_b = (lane % 16) < 8
        cosa, sina, cosb, sinb = cosa_ref[...], sina_ref[...], cosb_ref[...], sinb_ref[...]
        a_cols = [_rope_cols(pa[:, c * LANES:(c + 1) * LANES], cosa, sina, 16, lo_a) for c in range(3)]
        b_cols = [_rope_cols(pb[:, c * LANES:(c + 1) * LANES], cosb, sinb, 8, lo_b) for c in range(4)]
    a_scale = A_HEAD_DIM ** -0.5 * LOG2E
    b_scale = B_QK_DIM ** -0.5 * LOG2E
    aq0, aq1 = _pair_heads(a_cols[0] * a_scale, a_cols[1] * a_scale, lane < 64)
    qkv_ref[:, QKV_AQ:QKV_AQ + 128] = aq0.astype(BF16)
    qkv_ref[:, QKV_AQ + 128:QKV_AQ + 256] = aq1.astype(BF16)
    qkv_ref[:, QKV_AV:QKV_AV + 128] = p[:, COL_AV:COL_BQ].astype(BF16)
    qkv_ref[:, QKV_AV + 128:QKV_AV + 256] = jnp.ones((p.shape[0], LANES), BF16)
    qkv_ref[:, QKV_BQ:QKV_BQ + 128] = (b_cols[0] * b_scale).astype(BF16)
    qkv_ref[:, QKV_BQ + 128:QKV_BQ + 256] = (b_cols[1] * b_scale).astype(BF16)
    qkv_ref[:, QKV_BK:QKV_BK + 128] = b_cols[2].astype(BF16)
    qkv_ref[:, QKV_BK + 128:QKV_BK + 256] = b_cols[3].astype(BF16)
    qkv_ref[:, QKV_BV:QKV_BV + 256] = p[:, COL_BV:COL_CX].astype(BF16)
    qkv_ref[:, QKV_AK:QKV_AK + 128] = a_cols[2].astype(BF16)
    cd_ref[:, 0:512] = p[:, COL_DZ:IN_WIDTH]
    cd_ref[:, 512:768] = p[:, COL_CX:COL_DZ]


def _inproj(x, mods, layer, gpre, win, gqk, seg, rope, own, *, latent, seq):
    n = x.shape[0]
    tm = TOKEN_TILE
    tiles_per_seq = seq // tm if latent else 1
    if latent:
        mod_row = lambda i: 1 + i // tiles_per_seq
    else:
        mod_row = lambda i: 0
    in_specs = [
        pl.BlockSpec((tm, D_MODEL), lambda i: (i, 0)),
        pl.BlockSpec((None, None, N_MOD, D_MODEL), lambda i: (layer, mod_row(i), 0, 0)),
        pl.BlockSpec((None, 1, D_MODEL), lambda i: (layer, 0, 0)),
        pl.BlockSpec((None, D_MODEL, IN_WIDTH), lambda i: (layer, 0, 0), pipeline_mode=pl.Buffered(1)),
        pl.BlockSpec((None, 1, 384), lambda i: (layer, 0, 0)),
        pl.BlockSpec((384, 384), lambda i: (0, 0)),
    ]
    args = [x, mods, gpre, win, gqk, seg]
    out_specs = [pl.BlockSpec((tm, QKV_WIDTH), lambda i: (i, 0)),
                 pl.BlockSpec((tm, CD_WIDTH), lambda i: (i, 0))]
    out_shape = [jax.ShapeDtypeStruct((n, QKV_WIDTH), BF16), jax.ShapeDtypeStruct((n, CD_WIDTH), F32)]
    aliases = {}
    if latent:
        in_specs += [pl.BlockSpec((tm, LANES), lambda i: (i % tiles_per_seq, 0))] * 4
        args += list(rope)
    else:
        for arr, width in zip(own, OWN_WIDTHS):
            aliases[len(args)] = len(out_shape)
            in_specs.append(pl.BlockSpec(memory_space=pl.ANY))
            args.append(arr)
            out_specs.append(pl.BlockSpec((tm // seq, None, seq, width), lambda i: (i, layer, 0, 0)))
            out_shape.append(jax.ShapeDtypeStruct(arr.shape, arr.dtype))
    return pl.pallas_call(
        functools.partial(_inproj_kernel, latent=latent, n_aliased=len(aliases)),
        grid=(n // tm,),
        in_specs=in_specs,
        out_specs=out_specs,
        out_shape=out_shape,
        input_output_aliases=aliases,
        compiler_params=pltpu.CompilerParams(
            dimension_semantics=("arbitrary",), vmem_limit_bytes=VMEM_LIMIT_BYTES),
        name="inproj_latent" if latent else "inproj_context",
    )(*args)


def _scores(qm, k_segs):
    return [_dot_nt(qm, k) for k in k_segs]


def _numerators(ss):
    m = functools.reduce(jnp.maximum, [jnp.max(s, axis=-1, keepdims=True) for s in ss])
    return [jnp.exp2(s - m) for s in ss]


def _rowsum(ps):
    return functools.reduce(jnp.add, [jnp.sum(p, axis=-1, keepdims=True) for p in ps])


def _pool_windows(xh, tq):
    n = tq + 2 * HALO
    up = lambda a, k: pltpu.roll(a, n - k, axis=0)
    a2 = xh + up(xh, 1)
    a4 = a2 + up(a2, 2)
    a8 = a4 + up(a4, 4)
    a16 = a8 + up(a8, 8)
    w2 = up(a2, HALO - 1)[0:tq]
    w4 = up(a4, HALO - 2)[0:tq]
    w8 = up(a8, HALO - 4)[0:tq]
    w16 = a16[0:tq]
    return w2, w4, w8, w16


def _mixer_kernel(*refs, latent, lam_init, tq, n_q, n_sub, seq, ahead):
    if latent:
        (aq_ref, ak_ref, av_ref, bq_ref, bk_ref, bv_ref, cak_ref, cav_ref, cbk_ref, cbv_ref,
         cx_ref, cxp_ref, cxn_ref, dz_ref, icnt_ref, lamp_ref, subln_ref, cw_ref, cscale_ref,
         gv_ref, ws_ref, bs_ref, out_ref) = refs
    else:
        (aq_ref, ak_ref, av_ref, bq_ref, bk_ref, bv_ref,
         cx_ref, dz_ref, icnt_ref, lamp_ref, subln_ref, cw_ref, cscale_ref,
         gv_ref, ws_ref, bs_ref, out_ref) = refs
        cak_ref = cav_ref = cbk_ref = cbv_ref = cxp_ref = cxn_ref = None
    i = pl.program_id(1)
    lane = lax.broadcasted_iota(jnp.int32, (1, LANES), 1)
    half0 = lane < 64

    def pool_mixer(r):
        rows = slice(r * tq, (r + 1) * tq)
        cur = cx_ref[rows, :]
        if latent:
            prev = jnp.where(i == 0, 0.0, cxp_ref[...])
            nxt = jnp.where(i == n_q - 1, 0.0, cxn_ref[...])
        else:
            prev = nxt = jnp.zeros((HALO, GROUP_W), F32)
        xh = jnp.concatenate([prev, cur, nxt], axis=0)
        w2, w4, _, _ = _pool_windows(xh[:, 0:LANES], tq)
        _, _, w8, w16 = _pool_windows(xh[:, LANES:2 * LANES], tq)
        sums = jnp.concatenate([jnp.where(half0, w2, w4), jnp.where(half0, w8, w16)], axis=1)
        pooled = sums * icnt_ref[...] - cur
        yc = _dot(pooled.astype(BF16), cw_ref[...]) * cscale_ref[...]
        out_ref[rows, 512:768] = yc.astype(BF16)

    lane2 = lax.broadcasted_iota(jnp.int32, (1, GROUP_W), 1)

    def gate_mixer(n):
        rows = slice(n * CHUNK, (n + 1) * CHUNK)
        z = jax.nn.gelu(dz_ref[rows, :])
        vc = _rms(z[:, GROUP_W:2 * GROUP_W], gv_ref[...])
        stacked = jnp.concatenate(
            [jnp.where((lane2 >= 64 * g) & (lane2 < 64 * (g + 1)), vc, 0.0).astype(BF16) for g in range(4)],
            axis=0)
        sg = _dot(ws_ref[...], stacked) + bs_ref[...]
        out_ref[rows, 768:1024] = (z[:, 0:GROUP_W] * sg).astype(BF16)

    lp = lamp_ref[...]
    lam = (jnp.exp(jnp.sum(lp[0:1] * lp[1:2], axis=-1, keepdims=True))
           - jnp.exp(jnp.sum(lp[2:3] * lp[3:4], axis=-1, keepdims=True)) + lam_init)

    def key_segments(ctx_ref, own_ref, r, cols):
        return ([ctx_ref[:, cols]] if latent else []) + [own_ref[r * seq:(r + 1) * seq, cols]]

    def head_queries(t):
        r, kind, c, u = heads[t]
        q_ref = aq_ref if kind == "a" else bq_ref
        qc = q_ref[r * tq:(r + 1) * tq, c * LANES:(c + 1) * LANES]
        zero = jnp.zeros_like(qc)
        if kind == "a":
            return [jnp.where(half0 if u == 0 else ~half0, qc, zero)]
        sels = [(lane >= 64 * u + 32 * j) & (lane < 64 * u + 32 * j + 32) for j in range(2)]
        return [jnp.where(sel, qc, zero) for sel in sels]

    def head_scores(t):
        r, kind, c, _ = heads[t]
        if kind == "a":
            k_segs = key_segments(cak_ref, ak_ref, r, slice(0, LANES))
        else:
            k_segs = key_segments(cbk_ref, bk_ref, r, slice(c * LANES, (c + 1) * LANES))
        return [_scores(qm, k_segs) for qm in head_queries(t)]

    def head_weights(t, score_sets):
        kind = heads[t][1]
        if kind == "a":
            return [p.astype(BF16) for p in _numerators(score_sets[0])], None
        p0, p1 = _numerators(score_sets[0]), _numerators(score_sets[1])
        l0 = _rowsum(p0)
        ratio = lam * l0 / _rowsum(p1)
        return [(a - b * ratio).astype(BF16) for a, b in zip(p0, p1)], l0

    def head_output(t, weights):
        r, kind, c, u = heads[t]
        ws, l0 = weights
        if kind == "a":
            a_v = key_segments(cav_ref, av_ref, r, slice(0, 2 * LANES))
            o = functools.reduce(jnp.add, [_dot(w, v) for w, v in zip(ws, a_v)])
            return o[:, 0:LANES] / o[:, LANES:2 * LANES]
        b_v = key_segments(cbv_ref, bv_ref, r, slice(0, 2 * LANES))
        o = functools.reduce(jnp.add, [_dot(w, v) for w, v in zip(ws, b_v)])
        o = o[:, c * LANES:(c + 1) * LANES] / l0
        valid = half0 if u == 0 else ~half0
        ms = jnp.sum(jnp.where(valid, o * o, 0.0), axis=-1, keepdims=True) * (1.0 / 64)
        return o * lax.rsqrt(ms + RMS_EPS)

    heads = [(r, kind, c, u) for r in range(n_sub) for c in range(2) for u in range(2) for kind in ("a", "b")]
    side_work = ([functools.partial(pool_mixer, r) for r in range(n_sub)]
                 + [functools.partial(gate_mixer, n) for n in range(n_sub * tq // CHUNK)])
    outs = {}
    if ahead is None:
        scores = [head_scores(t) for t in range(len(heads))]
        for work in side_work:
            work()
        weights = [head_weights(t, scores[t]) for t in range(len(heads))]
        for t in range(len(heads)):
            outs[heads[t]] = head_output(t, weights[t])
    else:
        pending = [head_scores(t) for t in range(min(ahead, len(heads)))]
        for t in range(len(heads)):
            if t + ahead < len(heads):
                pending.append(head_scores(t + ahead))
            if side_work:
                side_work.pop(0)()
            outs[heads[t]] = head_output(t, head_weights(t, pending.pop(0)))
        for work in side_work:
            work()
    b_gain = subln_ref[...] * (1.0 - lam_init)
    for r in range(n_sub):
        rows = slice(r * tq, (r + 1) * tq)
        col = lambda kind, c: jnp.where(half0, outs[(r, kind, c, 0)], outs[(r, kind, c, 1)])
        a0, a1 = _unpair_heads(col("a", 0), col("a", 1), half0)
        out_ref[rows, 0:128] = a0.astype(BF16)
        out_ref[rows, 128:256] = a1.astype(BF16)
        out_ref[rows, 256:384] = (col("b", 0) * b_gain).astype(BF16)
        out_ref[rows, 384:512] = (col("b", 1) * b_gain).astype(BF16)


def _mixers(qkv, cd, ctx, icnt, lamp, subln, cw, cscale, gv, ws, bs, layer, *, latent, seq, lam_init):
    n = qkv.shape[0]
    n_batch = n // seq
    tq = min(Q_TILE, seq)
    n_q = seq // tq
    n_sub = 1 if latent else CONTEXT_SEQS_PER_STEP
    rows_q = n_sub * tq
    rows_k = n_sub * seq
    row = lambda b, i: b * n_q + i
    row8 = tq // SUBLANES
    last8 = n // SUBLANES - 1
    in_specs = [
        pl.BlockSpec((rows_q, 256), lambda b, i: (row(b, i), QKV_AQ // 256)),
        pl.BlockSpec((rows_k, 128), lambda b, i: (b, QKV_AK // 128)),
        pl.BlockSpec((rows_k, 256), lambda b, i: (b, QKV_AV // 256)),
        pl.BlockSpec((rows_q, 256), lambda b, i: (row(b, i), QKV_BQ // 256)),
        pl.BlockSpec((rows_k, 256), lambda b, i: (b, QKV_BK // 256)),
        pl.BlockSpec((rows_k, 256), lambda b, i: (b, QKV_BV // 256)),
    ]
    args = [qkv] * 6
    if latent:
        for w in (128, 256, 256, 256):
            in_specs.append(pl.BlockSpec((None, None, 256, w), lambda b, i: (b, layer, 0, 0)))
        args += list(ctx)
    in_specs.append(pl.BlockSpec((rows_q, 256), lambda b, i: (row(b, i), 2)))
    args.append(cd)
    if latent:
        in_specs += [
            pl.BlockSpec((HALO, 256), lambda b, i: (jnp.maximum(row(b, i) * row8 - 1, 0), 2)),
            pl.BlockSpec((HALO, 256), lambda b, i: (jnp.minimum((row(b, i) + 1) * row8, last8), 2)),
        ]
        args += [cd, cd]
    in_specs += [
        pl.BlockSpec((rows_q, 512), lambda b, i: (row(b, i), 0)),
        pl.BlockSpec((tq, 256), lambda b, i: (i, 0)),
        pl.BlockSpec((None, SUBLANES, LANES), lambda b, i: (layer, 0, 0)),
        pl.BlockSpec((None, 1, LANES), lambda b, i: (layer, 0, 0)),
        pl.BlockSpec((None, 256, 256), lambda b, i: (layer, 0, 0)),
        pl.BlockSpec((None, 1, 256), lambda b, i: (layer, 0, 0)),
        pl.BlockSpec((None, 1, 256), lambda b, i: (layer, 0, 0)),
        pl.BlockSpec((None, CHUNK, 4 * CHUNK), lambda b, i: (layer, 0, 0)),
        pl.BlockSpec((None, CHUNK, 256), lambda b, i: (layer, 0, 0)),
    ]
    args += [cd, icnt, lamp, subln, cw, cscale, gv, ws, bs]
    return pl.pallas_call(
        functools.partial(_mixer_kernel, latent=latent, lam_init=lam_init, tq=tq, n_q=n_q, n_sub=n_sub, seq=seq,
                          ahead=1 if latent else None),
        grid=(n_batch // n_sub, n_q),
        in_specs=in_specs,
        out_specs=pl.BlockSpec((rows_q, D_MODEL), lambda b, i: (row(b, i), 0)),
        out_shape=jax.ShapeDtypeStruct((n, D_MODEL), BF16),
        compiler_params=pltpu.CompilerParams(
            dimension_semantics=("arbitrary", "arbitrary"), vmem_limit_bytes=VMEM_LIMIT_BYTES),
        name="mixers_latent" if latent else "mixers_context",
    )(*args)


def _ffn_kernel(mix_ref, x_ref, mod_ref, wout_ref, gpm_ref, gpf_ref, gpo_ref, wg_ref, wu_ref, wd_ref,
                out_ref, acc_ref):
    y = _dot(mix_ref[...], wout_ref[...])
    x1 = x_ref[...] + mod_ref[2:3, :] * _rms(y, gpm_ref[...])
    h = (_rms(x1, gpf_ref[...]) * (1.0 + mod_ref[4:5, :]) + mod_ref[3:4, :]).astype(BF16)
    for c in range(N_FF_CHUNKS):
        cols = slice(c * FF_CHUNK, (c + 1) * FF_CHUNK)
        g = _dot(h, wg_ref[:, cols])
        u = _dot(h, wu_ref[:, cols])
        part = _dot((jax.nn.silu(g) * u).astype(BF16), wd_ref[cols, :])
        if c == 0:
            acc_ref[...] = part
        else:
            acc_ref[...] += part
    out_ref[...] = x1 + mod_ref[5:6, :] * _rms(acc_ref[...], gpo_ref[...])


def _outproj_ffn(mixed, x, mods, layer, wout, gpm, gpf, gpo, wg, wu, wd, *, latent, seq):
    n = x.shape[0]
    tm = FFN_TOKEN_TILE
    tiles_per_seq = seq // tm if latent else 1
    if latent:
        mod_row = lambda i: 1 + i // tiles_per_seq
    else:
        mod_row = lambda i: 0
    const = dict(pipeline_mode=pl.Buffered(1))
    gain = lambda: pl.BlockSpec((None, 1, D_MODEL), lambda i: (layer, 0, 0))
    return pl.pallas_call(
        _ffn_kernel,
        grid=(n // tm,),
        in_specs=[
            pl.BlockSpec((tm, D_MODEL), lambda i: (i, 0)),
            pl.BlockSpec((tm, D_MODEL), lambda i: (i, 0)),
            pl.BlockSpec((None, None, N_MOD, D_MODEL), lambda i: (layer, mod_row(i), 0, 0)),
            pl.BlockSpec((None, D_MODEL, D_MODEL), lambda i: (layer, 0, 0), **const),
            gain(), gain(), gain(),
            pl.BlockSpec((None, D_MODEL, D_FF), lambda i: (layer, 0, 0), **const),
            pl.BlockSpec((None, D_MODEL, D_FF), lambda i: (layer, 0, 0), **const),
            pl.BlockSpec((None, D_FF, D_MODEL), lambda i: (layer, 0, 0), **const),
        ],
        out_specs=pl.BlockSpec((tm, D_MODEL), lambda i: (i, 0)),
        out_shape=jax.ShapeDtypeStruct((n, D_MODEL), F32),
        scratch_shapes=[pltpu.VMEM((tm, D_MODEL), F32)],
        compiler_params=pltpu.CompilerParams(
            dimension_semantics=("arbitrary",), vmem_limit_bytes=VMEM_LIMIT_BYTES),
        name="outproj_ffn_latent" if latent else "outproj_ffn_context",
    )(mixed, x, mods, wout, gpm, gpf, gpo, wg, wu, wd)


def _rope_tables(seq):
    t = np.arange(seq)
    pos = {"row": (t // GRID_W).astype(np.float64), "col": (t % GRID_W).astype(np.float64)}

    def table(head_dim):
        half = head_dim // 2
        inv_freq = ROPE_THETA ** (-np.arange(0, half, 2, dtype=np.float64) / half)
        lane = np.arange(head_dim)
        within = lane % half
        freq = inv_freq[within % (half // 2)]
        p = np.where(lane[None, :] < half, pos["row"][:, None], pos["col"][:, None])
        ang = p * freq[None, :]
        sign = np.where(within < half // 2, -1.0, 1.0)
        reps = LANES // head_dim
        return (np.tile(np.cos(ang), (1, reps)).astype(np.float32),
                np.tile(np.sin(ang) * sign[None, :], (1, reps)).astype(np.float32))

    cosa, sina = table(A_HEAD_DIM)
    cosb, sinb = table(B_QK_DIM)
    return cosa, sina, cosb, sinb


def _inv_count_table(seq):
    t = np.arange(seq)
    cols = []
    for win in POOL_WINDOWS:
        cnt = np.minimum(t + win // 2, seq) - np.maximum(t - win // 2, 0)
        cols.append(np.repeat((1.0 / cnt)[:, None], 64, axis=1))
    return np.concatenate(cols, axis=1).astype(np.float32)


def _segment_mean_matrix():
    idx = np.arange(384) // A_HEAD_DIM
    return (idx[:, None] == idx[None, :]).astype(np.float32) / A_HEAD_DIM


def kernel(x_prompt, x_sample, cache_a_k, cache_a_v, cache_b_k, cache_b_v, c, c_ctx, w_mod, b_mod, g_pre_mix, g_post_mix, g_pre_ffn, g_post_ffn, w_in, w_out, a_q_norm, a_k_norm, b_lq1, b_lk1, b_lq2, b_lk2, b_subln, c_w, c_scale, d_v_norm, d_ws, d_bs, w_gate, w_up, w_down):
    batch, seq, _ = x_prompt.shape
    dec_batch, dec_seq, _ = x_sample.shape
    past = cache_a_k.shape[2]

    win = w_in.astype(BF16)
    wout = w_out.astype(BF16)
    wg = w_gate.astype(BF16)
    wu = w_up.astype(BF16)
    wd = w_down.astype(BF16)
    gqk = jnp.concatenate([jnp.tile(a_q_norm, (1, 4)), jnp.tile(a_k_norm, (1, 2))], axis=1).reshape(DEPTH, 1, 384)
    seg = jnp.asarray(_segment_mean_matrix(), BF16)
    lamp = jnp.stack([b_lq1, b_lk1, b_lq2, b_lk2], axis=1)
    lamp = jnp.pad(lamp, ((0, 0), (0, SUBLANES - 4), (0, LANES - B_QK_DIM)))
    subln = jnp.tile(b_subln, (1, 2)).reshape(DEPTH, 1, LANES)
    eye = jnp.eye(4, dtype=F32)
    cw = (eye[None, :, None, :, None] * c_w[:, :, :, None, :]).reshape(DEPTH, 256, 256).astype(BF16)
    cscale = c_scale.reshape(DEPTH, 1, GROUP_W)
    gv = d_v_norm.reshape(DEPTH, 1, GROUP_W)
    ws = d_ws.transpose(0, 2, 1, 3).reshape(DEPTH, CHUNK, 4 * CHUNK).astype(BF16)
    bs = jnp.repeat(d_bs.transpose(0, 2, 1), 64, axis=2)
    gpre = g_pre_mix.reshape(DEPTH, 1, D_MODEL)
    gpm = g_post_mix.reshape(DEPTH, 1, D_MODEL)
    gpf = g_pre_ffn.reshape(DEPTH, 1, D_MODEL)
    gpo = g_post_ffn.reshape(DEPTH, 1, D_MODEL)
    rope = tuple(jnp.asarray(t) for t in _rope_tables(dec_seq))
    icnt_ctx = jnp.asarray(_inv_count_table(seq))
    icnt_lat = jnp.asarray(_inv_count_table(dec_seq))
    flat = lambda a, w: a.reshape(dec_batch, DEPTH, past, w).astype(BF16)
    ctx = (flat(cache_a_k, 128),
           jnp.concatenate([flat(cache_a_v, 128), jnp.ones((dec_batch, DEPTH, past, 128), BF16)], axis=-1),
           flat(cache_b_k, 256), flat(cache_b_v, 256))

    cond8 = jnp.concatenate([c_ctx[None, :], c, jnp.zeros((SUBLANES - 1 - dec_batch, D_MODEL), F32)], axis=0)
    mods = _modulation(cond8, w_mod, b_mod)

    yp = x_prompt.reshape(batch * seq, D_MODEL)
    ys = x_sample.reshape(dec_batch * dec_seq, D_MODEL)
    own = [jnp.zeros((batch, DEPTH, seq, w), F32) for w in OWN_WIDTHS]
    for layer in range(DEPTH):
        lam_init = 0.8 - 0.6 * math.exp(-0.3 * layer)
        mix_args = (lamp, subln, cw, cscale, gv, ws, bs, layer)
        ffn_args = (mods, layer, wout, gpm, gpf, gpo, wg, wu, wd)
        qkv, cd, *own = _inproj(yp, mods, layer, gpre, win, gqk, seg, None, own, latent=False, seq=seq)
        mixed = _mixers(qkv, cd, None, icnt_ctx, *mix_args, latent=False, seq=seq, lam_init=lam_init)
        yp = _outproj_ffn(mixed, yp, *ffn_args, latent=False, seq=seq)
        qkv, cd = _inproj(ys, mods, layer, gpre, win, gqk, seg, rope, None, latent=True, seq=dec_seq)
        mixed = _mixers(qkv, cd, ctx, icnt_lat, *mix_args, latent=True, seq=dec_seq, lam_init=lam_init)
        ys = _outproj_ffn(mixed, ys, *ffn_args, latent=True, seq=dec_seq)

    new_a_k, new_a_v, new_b_k, new_b_v = own
    lead = (batch, DEPTH, seq)
    return (yp.reshape(batch, seq, D_MODEL), ys.reshape(dec_batch, dec_seq, D_MODEL),
            new_a_k.reshape(lead + (2, 64)), new_a_v.reshape(lead + (2, 64)),
            new_b_k.reshape(lead + (4, 2, 32)), new_b_v.reshape(lead + (4, 64)))
```

```python
import functools
import math

import numpy as np
import jax
import jax.numpy as jnp
from jax import lax
from jax.experimental import pallas as pl
from jax.experimental.pallas import tpu as pltpu

F32 = jnp.float32
BF16 = jnp.bfloat16

D_MODEL = 1024
DEPTH = 4
GRID_W = 64
GROUP_W = 256
A_HEAD_DIM = 64
B_QK_DIM = 32
POOL_WINDOWS = (2, 4, 8, 16)
CHUNK = 128
D_FF = 2816
N_MOD = 6
RMS_EPS = 1e-6
LOG2E = math.log2(math.e)
ROPE_THETA = 10000.0
IN_WIDTH = 2048

LANES = 128
SUBLANES = 8
VMEM_LIMIT_BYTES = 56 * 1024 * 1024

COL_AQ, COL_AK, COL_AV, COL_BQ, COL_BK, COL_BV, COL_CX, COL_DZ = 0, 256, 384, 512, 768, 1024, 1280, 1536
QKV_AQ, QKV_AV, QKV_BQ, QKV_BK, QKV_BV, QKV_AK = 0, 256, 512, 768, 1024, 1536
QKV_WIDTH = 1664
CD_WIDTH = 768
OWN_WIDTHS = (128, 128, 256, 256)
FF_CHUNK = 256
N_FF_CHUNKS = D_FF // FF_CHUNK

TOKEN_TILE = 1024
FFN_TOKEN_TILE = 1024
Q_TILE = 512
CONTEXT_SEQS_PER_STEP = 4
MOD_COL_TILE = 1536
HALO = 8

def _rms(x, g):
    ms = jnp.mean(x * x, axis=-1, keepdims=True)
    return x * lax.rsqrt(ms + RMS_EPS) * g


def _dot(a, b):
    return jnp.dot(a, b, preferred_element_type=F32)


def _dot_nt(a, b):
    return lax.dot_general(a, b, (((1,), (1,)), ((), ())), preferred_element_type=F32)


def _pair_heads(c0, c1, half0):
    return jnp.where(half0, c0, c1), pltpu.roll(jnp.where(half0, c1, c0), 64, axis=1)


def _unpair_heads(d0, d1, half0):
    r = pltpu.roll(d1, 64, axis=1)
    return jnp.where(half0, d0, r), jnp.where(half0, r, d0)


def _mod_kernel(cond_ref, w_ref, b_ref, out_ref):
    s = jax.nn.silu(cond_ref[...]).astype(BF16)
    out_ref[...] = _dot(s, w_ref[...].astype(BF16)) + b_ref[...]


def _modulation(cond8, w_mod, b_mod):
    n_col = (N_MOD * D_MODEL) // MOD_COL_TILE
    out = pl.pallas_call(
        _mod_kernel,
        grid=(DEPTH, n_col),
        in_specs=[
            pl.BlockSpec((SUBLANES, D_MODEL), lambda l, j: (0, 0)),
            pl.BlockSpec((None, D_MODEL, MOD_COL_TILE), lambda l, j: (l, 0, j)),
            pl.BlockSpec((None, 1, MOD_COL_TILE), lambda l, j: (l, 0, j)),
        ],
        out_specs=pl.BlockSpec((None, SUBLANES, MOD_COL_TILE), lambda l, j: (l, 0, j)),
        out_shape=jax.ShapeDtypeStruct((DEPTH, SUBLANES, N_MOD * D_MODEL), F32),
        compiler_params=pltpu.CompilerParams(
            dimension_semantics=("arbitrary", "arbitrary"), vmem_limit_bytes=VMEM_LIMIT_BYTES),
        name="modulation",
    )(cond8, w_mod, b_mod.reshape(DEPTH, 1, N_MOD * D_MODEL))
    return out.reshape(DEPTH, SUBLANES, N_MOD, D_MODEL)


def _rope_cols(x, cos, sin, shift, lo_mask):
    fwd = pltpu.roll(x, LANES - shift, axis=1)
    bwd = pltpu.roll(x, shift, axis=1)
    return x * cos + jnp.where(lo_mask, fwd, bwd) * sin


def _inproj_kernel(*refs, latent, n_aliased):
    if latent:
        (x_ref, mod_ref, gpre_ref, win_ref, gqk_ref, seg_ref, cosa_ref, sina_ref, cosb_ref, sinb_ref,
         qkv_ref, cd_ref) = refs
    else:
        (x_ref, mod_ref, gpre_ref, win_ref, gqk_ref, seg_ref) = refs[:6]
        qkv_ref, cd_ref, oak_ref, oav_ref, obk_ref, obv_ref = refs[6 + n_aliased:]
    h = _rms(x_ref[...], gpre_ref[...]) * (1.0 + mod_ref[1:2, :]) + mod_ref[0:1, :]
    p = _dot(h.astype(BF16), win_ref[...])

    pa = p[:, COL_AQ:COL_AV]
    msq = _dot((pa * pa).astype(BF16), seg_ref[...])
    pa = pa * lax.rsqrt(msq + RMS_EPS) * gqk_ref[...]
    pb = p[:, COL_BQ:COL_BV]
    lane = lax.broadcasted_iota(jnp.int32, (1, LANES), 1)
    if not latent:
        oak_ref[...] = pa[:, COL_AK:COL_AV].reshape(oak_ref.shape)
        oav_ref[...] = p[:, COL_AV:COL_BQ].reshape(oav_ref.shape)
        obk_ref[...] = p[:, COL_BK:COL_BV].reshape(obk_ref.shape)
        obv_ref[...] = p[:, COL_BV:COL_CX].reshape(obv_ref.shape)
        a_cols = [pa[:, c * LANES:(c + 1) * LANES] for c in range(3)]
        b_cols = [pb[:, c * LANES:(c + 1) * LANES] for c in range(4)]
    else:
        lo_a = (lane % 32) < 16
        lo_b = (lane % 16) < 8
        cosa, sina, cosb, sinb = cosa_ref[...], sina_ref[...], cosb_ref[...], sinb_ref[...]
        a_cols = [_rope_cols(pa[:, c * LANES:(c + 1) * LANES], cosa, sina, 16, lo_a) for c in range(3)]
        b_cols = [_rope_cols(pb[:, c * LANES:(c + 1) * LANES], cosb, sinb, 8, lo_b) for c in range(4)]
    a_scale = A_HEAD_DIM ** -0.5 * LOG2E
    b_scale = B_QK_DIM ** -0.5 * LOG2E
    aq0, aq1 = _pair_heads(a_cols[0] * a_scale, a_cols[1] * a_scale, lane < 64)
    qkv_ref[:, QKV_AQ:QKV_AQ + 128] = aq0.astype(BF16)
    qkv_ref[:, QKV_AQ + 128:QKV_AQ + 256] = aq1.astype(BF16)
    qkv_ref[:, QKV_AV:QKV_AV + 128] = p[:, COL_AV:COL_BQ].astype(BF16)
    ones = jnp.ones((p.shape[0], LANES), BF16)
    qkv_ref[:, QKV_AV + 128:QKV_AV + 256] = ones
    qkv_ref[:, QKV_BQ:QKV_BQ + 128] = (b_cols[0] * b_scale).astype(BF16)
    qkv_ref[:, QKV_BQ + 128:QKV_BQ + 256] = (b_cols[1] * b_scale).astype(BF16)
    qkv_ref[:, QKV_BK:QKV_BK + 128] = b_cols[2].astype(BF16)
    qkv_ref[:, QKV_BK + 128:QKV_BK + 256] = b_cols[3].astype(BF16)
    for c in range(2):
        qkv_ref[:, QKV_BV + 256 * c:QKV_BV + 256 * c + 128] = p[:, COL_BV + 128 * c:COL_BV + 128 * (c + 1)].astype(BF16)
        qkv_ref[:, QKV_BV + 256 * c + 128:QKV_BV + 256 * (c + 1)] = ones
    qkv_ref[:, QKV_AK:QKV_AK + 128] = a_cols[2].astype(BF16)
    cd_ref[:, 0:512] = p[:, COL_DZ:IN_WIDTH]
    cd_ref[:, 512:768] = p[:, COL_CX:COL_DZ]


def _inproj(x, mods, layer, gpre, win, gqk, seg, rope, own, *, latent, seq):
    n = x.shape[0]
    tm = TOKEN_TILE
    tiles_per_seq = seq // tm if latent else 1
    if latent:
        mod_row = lambda i: 1 + i // tiles_per_seq
    else:
        mod_row = lambda i: 0
    in_specs = [
        pl.BlockSpec((tm, D_MODEL), lambda i: (i, 0)),
        pl.BlockSpec((None, None, N_MOD, D_MODEL), lambda i: (layer, mod_row(i), 0, 0)),
        pl.BlockSpec((None, 1, D_MODEL), lambda i: (layer, 0, 0)),
        pl.BlockSpec((None, D_MODEL, IN_WIDTH), lambda i: (layer, 0, 0), pipeline_mode=pl.Buffered(1)),
        pl.BlockSpec((None, 1, 384), lambda i: (layer, 0, 0)),
        pl.BlockSpec((384, 384), lambda i: (0, 0)),
    ]
    args = [x, mods, gpre, win, gqk, seg]
    out_specs = [pl.BlockSpec((tm, QKV_WIDTH), lambda i: (i, 0)),
                 pl.BlockSpec((tm, CD_WIDTH), lambda i: (i, 0))]
    out_shape = [jax.ShapeDtypeStruct((n, QKV_WIDTH), BF16), jax.ShapeDtypeStruct((n, CD_WIDTH), F32)]
    aliases = {}
    if latent:
        in_specs += [pl.BlockSpec((tm, LANES), lambda i: (i % tiles_per_seq, 0))] * 4
        args += list(rope)
    else:
        for arr, width in zip(own, OWN_WIDTHS):
            aliases[len(args)] = len(out_shape)
            in_specs.append(pl.BlockSpec(memory_space=pl.ANY))
            args.append(arr)
            out_specs.append(pl.BlockSpec((tm // seq, None, seq, width), lambda i: (i, layer, 0, 0)))
            out_shape.append(jax.ShapeDtypeStruct(arr.shape, arr.dtype))
    return pl.pallas_call(
        functools.partial(_inproj_kernel, latent=latent, n_aliased=len(aliases)),
        grid=(n // tm,),
        in_specs=in_specs,
        out_specs=out_specs,
        out_shape=out_shape,
        input_output_aliases=aliases,
        compiler_params=pltpu.CompilerParams(
            dimension_semantics=("arbitrary",), vmem_limit_bytes=VMEM_LIMIT_BYTES),
        name="inproj_latent" if latent else "inproj_context",
    )(*args)


def _scores(qm, k_segs):
    return [_dot_nt(qm, k) for k in k_segs]


def _numerators(ss):
    m = functools.reduce(jnp.maximum, [jnp.max(s, axis=-1, keepdims=True) for s in ss])
    return [jnp.exp2(s - m) for s in ss]


def _rowsum(ps):
    return functools.reduce(jnp.add, [jnp.sum(p, axis=-1, keepdims=True) for p in ps])


def _pool_windows(xh, tq):
    n = tq + 2 * HALO
    up = lambda a, k: pltpu.roll(a, n - k, axis=0)
    a2 = xh + up(xh, 1)
    a4 = a2 + up(a2, 2)
    a8 = a4 + up(a4, 4)
    a16 = a8 + up(a8, 8)
    w2 = up(a2, HALO - 1)[0:tq]
    w4 = up(a4, HALO - 2)[0:tq]
    w8 = up(a8, HALO - 4)[0:tq]
    w16 = a16[0:tq]
    return w2, w4, w8, w16


def _mixer_kernel(*refs, latent, lam_init, tq, n_q, n_sub, seq, ahead):
    if latent:
        (aq_ref, ak_ref, av_ref, bq_ref, bk_ref, bv_ref, cak_ref, cav_ref, cbk_ref, cbv_ref,
         cx_ref, cxp_ref, cxn_ref, dz_ref, icnt_ref, lamp_ref, subln_ref, cw_ref, cscale_ref,
         gv_ref, ws_ref, bs_ref, out_ref) = refs
    else:
        (aq_ref, ak_ref, av_ref, bq_ref, bk_ref, bv_ref,
         cx_ref, dz_ref, icnt_ref, lamp_ref, subln_ref, cw_ref, cscale_ref,
         gv_ref, ws_ref, bs_ref, out_ref) = refs
        cak_ref = cav_ref = cbk_ref = cbv_ref = cxp_ref = cxn_ref = None
    i = pl.program_id(1)
    lane = lax.broadcasted_iota(jnp.int32, (1, LANES), 1)
    half0 = lane < 64

    def pool_mixer(r):
        rows = slice(r * tq, (r + 1) * tq)
        cur = cx_ref[rows, :]
        if latent:
            prev = jnp.where(i == 0, 0.0, cxp_ref[...])
            nxt = jnp.where(i == n_q - 1, 0.0, cxn_ref[...])
        else:
            prev = nxt = jnp.zeros((HALO, GROUP_W), F32)
        xh = jnp.concatenate([prev, cur, nxt], axis=0)
        w2, w4, _, _ = _pool_windows(xh[:, 0:LANES], tq)
        _, _, w8, w16 = _pool_windows(xh[:, LANES:2 * LANES], tq)
        sums = jnp.concatenate([jnp.where(half0, w2, w4), jnp.where(half0, w8, w16)], axis=1)
        pooled = sums * icnt_ref[...] - cur
        yc = _dot(pooled.astype(BF16), cw_ref[...]) * cscale_ref[...]
        out_ref[rows, 512:768] = yc.astype(BF16)

    lane2 = lax.broadcasted_iota(jnp.int32, (1, GROUP_W), 1)

    def gate_mixer(n):
        rows = slice(n * CHUNK, (n + 1) * CHUNK)
        z = jax.nn.gelu(dz_ref[rows, :])
        vc = _rms(z[:, GROUP_W:2 * GROUP_W], gv_ref[...])
        stacked = jnp.concatenate(
            [jnp.where((lane2 >= 64 * g) & (lane2 < 64 * (g + 1)), vc, 0.0).astype(BF16) for g in range(4)],
            axis=0)
        sg = _dot(ws_ref[...], stacked) + bs_ref[...]
        out_ref[rows, 768:1024] = (z[:, 0:GROUP_W] * sg).astype(BF16)

    lp = lamp_ref[...]
    lam = (jnp.exp(jnp.sum(lp[0:1] * lp[1:2], axis=-1, keepdims=True))
           - jnp.exp(jnp.sum(lp[2:3] * lp[3:4], axis=-1, keepdims=True)) + lam_init)

    def key_segments(ctx_ref, own_ref, r, cols):
        return ([ctx_ref[:, cols]] if latent else []) + [own_ref[r * seq:(r + 1) * seq, cols]]

    def head_queries(t):
        r, kind, c, u = heads[t]
        q_ref = aq_ref if kind == "a" else bq_ref
        qc = q_ref[r * tq:(r + 1) * tq, c * LANES:(c + 1) * LANES]
        zero = jnp.zeros_like(qc)
        if kind == "a":
            return [jnp.where(half0 if u == 0 else ~half0, qc, zero)]
        sels = [(lane >= 64 * u + 32 * j) & (lane < 64 * u + 32 * j + 32) for j in range(2)]
        return [jnp.where(sel, qc, zero) for sel in sels]

    def head_scores(t):
        r, kind, c, _ = heads[t]
        if kind == "a":
            k_segs = key_segments(cak_ref, ak_ref, r, slice(0, LANES))
        else:
            k_segs = key_segments(cbk_ref, bk_ref, r, slice(c * LANES, (c + 1) * LANES))
        return [_scores(qm, k_segs) for qm in head_queries(t)]

    def head_weights(t, score_sets):
        kind = heads[t][1]
        if kind == "a":
            return [p.astype(BF16) for p in _numerators(score_sets[0])], None
        return [[p.astype(BF16) for p in _numerators(ss)] for ss in score_sets], None

    def head_output(t, weights):
        r, kind, c, u = heads[t]
        ws, l0 = weights
        if kind == "a":
            a_v = key_segments(cav_ref, av_ref, r, slice(0, 2 * LANES))
            o = functools.reduce(jnp.add, [_dot(w, v) for w, v in zip(ws, a_v)])
            return o[:, 0:LANES] / o[:, LANES:2 * LANES]
        b_v = key_segments(cbv_ref, bv_ref, r, slice(2 * c * LANES, 2 * (c + 1) * LANES))
        o0, o1 = [functools.reduce(jnp.add, [_dot(w, v) for w, v in zip(wj, b_v)]) for wj in ws]
        o = o0[:, 0:LANES] / o0[:, LANES:2 * LANES] - lam * (o1[:, 0:LANES] / o1[:, LANES:2 * LANES])
        valid = half0 if u == 0 else ~half0
        ms = jnp.sum(jnp.where(valid, o * o, 0.0), axis=-1, keepdims=True) * (1.0 / 64)
        return o * lax.rsqrt(ms + RMS_EPS)

    heads = [(r, kind, c, u) for r in range(n_sub) for c in range(2) for u in range(2) for kind in ("a", "b")]
    side_work = ([functools.partial(pool_mixer, r) for r in range(n_sub)]
                 + [functools.partial(gate_mixer, n) for n in range(n_sub * tq // CHUNK)])
    outs = {}
    if ahead is None:
        scores = [head_scores(t) for t in range(len(heads))]
        for work in side_work:
            work()
        weights = [head_weights(t, scores[t]) for t in range(len(heads))]
        for t in range(len(heads)):
            outs[heads[t]] = head_output(t, weights[t])
    else:
        pending = [head_scores(t) for t in range(min(ahead, len(heads)))]
        for t in range(len(heads)):
            if t + ahead < len(heads):
                pending.append(head_scores(t + ahead))
            if side_work:
                side_work.pop(0)()
            outs[heads[t]] = head_output(t, head_weights(t, pending.pop(0)))
        for work in side_work:
            work()
    b_gain = subln_ref[...] * (1.0 - lam_init)
    for r in range(n_sub):
        rows = slice(r * tq, (r + 1) * tq)
        col = lambda kind, c: jnp.where(half0, outs[(r, kind, c, 0)], outs[(r, kind, c, 1)])
        a0, a1 = _unpair_heads(col("a", 0), col("a", 1), half0)
        out_ref[rows, 0:128] = a0.astype(BF16)
        out_ref[rows, 128:256] = a1.astype(BF16)
        out_ref[rows, 256:384] = (col("b", 0) * b_gain).astype(BF16)
        out_ref[rows, 384:512] = (col("b", 1) * b_gain).astype(BF16)


def _mixers(qkv, cd, ctx, icnt, lamp, subln, cw, cscale, gv, ws, bs, layer, *, latent, seq, lam_init):
    n = qkv.shape[0]
    n_batch = n // seq
    tq = min(Q_TILE, seq)
    n_q = seq // tq
    n_sub = 1 if latent else CONTEXT_SEQS_PER_STEP
    rows_q = n_sub * tq
    rows_k = n_sub * seq
    row = lambda b, i: b * n_q + i
    row8 = tq // SUBLANES
    last8 = n // SUBLANES - 1
    in_specs = [
        pl.BlockSpec((rows_q, 256), lambda b, i: (row(b, i), QKV_AQ // 256)),
        pl.BlockSpec((rows_k, 128), lambda b, i: (b, QKV_AK // 128)),
        pl.BlockSpec((rows_k, 256), lambda b, i: (b, QKV_AV // 256)),
        pl.BlockSpec((rows_q, 256), lambda b, i: (row(b, i), QKV_BQ // 256)),
        pl.BlockSpec((rows_k, 256), lambda b, i: (b, QKV_BK // 256)),
        pl.BlockSpec((rows_k, 512), lambda b, i: (b, QKV_BV // 512)),
    ]
    args = [qkv] * 6
    if latent:
        for w in (128, 256, 256, 512):
            in_specs.append(pl.BlockSpec((None, None, 256, w), lambda b, i: (b, layer, 0, 0)))
        args += list(ctx)
    in_specs.append(pl.BlockSpec((rows_q, 256), lambda b, i: (row(b, i), 2)))
    args.append(cd)
    if latent:
        in_specs += [
            pl.BlockSpec((HALO, 256), lambda b, i: (jnp.maximum(row(b, i) * row8 - 1, 0), 2)),
            pl.BlockSpec((HALO, 256), lambda b, i: (jnp.minimum((row(b, i) + 1) * row8, last8), 2)),
        ]
        args += [cd, cd]
    in_specs += [
        pl.BlockSpec((rows_q, 512), lambda b, i: (row(b, i), 0)),
        pl.BlockSpec((tq, 256), lambda b, i: (i, 0)),
        pl.BlockSpec((None, SUBLANES, LANES), lambda b, i: (layer, 0, 0)),
        pl.BlockSpec((None, 1, LANES), lambda b, i: (layer, 0, 0)),
        pl.BlockSpec((None, 256, 256), lambda b, i: (layer, 0, 0)),
        pl.BlockSpec((None, 1, 256), lambda b, i: (layer, 0, 0)),
        pl.BlockSpec((None, 1, 256), lambda b, i: (layer, 0, 0)),
        pl.BlockSpec((None, CHUNK, 4 * CHUNK), lambda b, i: (layer, 0, 0)),
        pl.BlockSpec((None, CHUNK, 256), lambda b, i: (layer, 0, 0)),
    ]
    args += [cd, icnt, lamp, subln, cw, cscale, gv, ws, bs]
    return pl.pallas_call(
        functools.partial(_mixer_kernel, latent=latent, lam_init=lam_init, tq=tq, n_q=n_q, n_sub=n_sub, seq=seq,
                          ahead=1 if latent else None),
        grid=(n_batch // n_sub, n_q),
        in_specs=in_specs,
        out_specs=pl.BlockSpec((rows_q, D_MODEL), lambda b, i: (row(b, i), 0)),
        out_shape=jax.ShapeDtypeStruct((n, D_MODEL), BF16),
        compiler_params=pltpu.CompilerParams(
            dimension_semantics=("arbitrary", "arbitrary"), vmem_limit_bytes=VMEM_LIMIT_BYTES),
        name="mixers_latent" if latent else "mixers_context",
    )(*args)


def _ffn_kernel(mix_ref, x_ref, mod_ref, wout_ref, gpm_ref, gpf_ref, gpo_ref, wg_ref, wu_ref, wd_ref,
                out_ref, acc_ref):
    y = _dot(mix_ref[...], wout_ref[...])
    x1 = x_ref[...] + mod_ref[2:3, :] * _rms(y, gpm_ref[...])
    h = (_rms(x1, gpf_ref[...]) * (1.0 + mod_ref[4:5, :]) + mod_ref[3:4, :]).astype(BF16)
    for c in range(N_FF_CHUNKS):
        cols = slice(c * FF_CHUNK, (c + 1) * FF_CHUNK)
        g = _dot(h, wg_ref[:, cols])
        u = _dot(h, wu_ref[:, cols])
        part = _dot((jax.nn.silu(g) * u).astype(BF16), wd_ref[cols, :])
        if c == 0:
            acc_ref[...] = part
        else:
            acc_ref[...] += part
    out_ref[...] = x1 + mod_ref[5:6, :] * _rms(acc_ref[...], gpo_ref[...])


def _outproj_ffn(mixed, x, mods, layer, wout, gpm, gpf, gpo, wg, wu, wd, *, latent, seq):
    n = x.shape[0]
    tm = FFN_TOKEN_TILE
    tiles_per_seq = seq // tm if latent else 1
    if latent:
        mod_row = lambda i: 1 + i // tiles_per_seq
    else:
        mod_row = lambda i: 0
    const = dict(pipeline_mode=pl.Buffered(1))
    gain = lambda: pl.BlockSpec((None, 1, D_MODEL), lambda i: (layer, 0, 0))
    return pl.pallas_call(
        _ffn_kernel,
        grid=(n // tm,),
        in_specs=[
            pl.BlockSpec((tm, D_MODEL), lambda i: (i, 0)),
            pl.BlockSpec((tm, D_MODEL), lambda i: (i, 0)),
            pl.BlockSpec((None, None, N_MOD, D_MODEL), lambda i: (layer, mod_row(i), 0, 0)),
            pl.BlockSpec((None, D_MODEL, D_MODEL), lambda i: (layer, 0, 0), **const),
            gain(), gain(), gain(),
            pl.BlockSpec((None, D_MODEL, D_FF), lambda i: (layer, 0, 0), **const),
            pl.BlockSpec((None, D_MODEL, D_FF), lambda i: (layer, 0, 0), **const),
            pl.BlockSpec((None, D_FF, D_MODEL), lambda i: (layer, 0, 0), **const),
        ],
        out_specs=pl.BlockSpec((tm, D_MODEL), lambda i: (i, 0)),
        out_shape=jax.ShapeDtypeStruct((n, D_MODEL), F32),
        scratch_shapes=[pltpu.VMEM((tm, D_MODEL), F32)],
        compiler_params=pltpu.CompilerParams(
            dimension_semantics=("arbitrary",), vmem_limit_bytes=VMEM_LIMIT_BYTES),
        name="outproj_ffn_latent" if latent else "outproj_ffn_context",
    )(mixed, x, mods, wout, gpm, gpf, gpo, wg, wu, wd)


def _rope_tables(seq):
    t = np.arange(seq)
    pos = {"row": (t // GRID_W).astype(np.float64), "col": (t % GRID_W).astype(np.float64)}

    def table(head_dim):
        half = head_dim // 2
        inv_freq = ROPE_THETA ** (-np.arange(0, half, 2, dtype=np.float64) / half)
        lane = np.arange(head_dim)
        within = lane % half
        freq = inv_freq[within % (half // 2)]
        p = np.where(lane[None, :] < half, pos["row"][:, None], pos["col"][:, None])
        ang = p * freq[None, :]
        sign = np.where(within < half // 2, -1.0, 1.0)
        reps = LANES // head_dim
        return (np.tile(np.cos(ang), (1, reps)).astype(np.float32),
                np.tile(np.sin(ang) * sign[None, :], (1, reps)).astype(np.float32))

    cosa, sina = table(A_HEAD_DIM)
    cosb, sinb = table(B_QK_DIM)
    return cosa, sina, cosb, sinb


def _inv_count_table(seq):
    t = np.arange(seq)
    cols = []
    for win in POOL_WINDOWS:
        cnt = np.minimum(t + win // 2, seq) - np.maximum(t - win // 2, 0)
        cols.append(np.repeat((1.0 / cnt)[:, None], 64, axis=1))
    return np.concatenate(cols, axis=1).astype(np.float32)


def _segment_mean_matrix():
    idx = np.arange(384) // A_HEAD_DIM
    return (idx[:, None] == idx[None, :]).astype(np.float32) / A_HEAD_DIM


def kernel(x_prompt, x_sample, cache_a_k, cache_a_v, cache_b_k, cache_b_v, c, c_ctx, w_mod, b_mod, g_pre_mix, g_post_mix, g_pre_ffn, g_post_ffn, w_in, w_out, a_q_norm, a_k_norm, b_lq1, b_lk1, b_lq2, b_lk2, b_subln, c_w, c_scale, d_v_norm, d_ws, d_bs, w_gate, w_up, w_down):
    batch, seq, _ = x_prompt.shape
    dec_batch, dec_seq, _ = x_sample.shape
    past = cache_a_k.shape[2]

    win = w_in.astype(BF16)
    wout = w_out.astype(BF16)
    wg = w_gate.astype(BF16)
    wu = w_up.astype(BF16)
    wd = w_down.astype(BF16)
    gqk = jnp.concatenate([jnp.tile(a_q_norm, (1, 4)), jnp.tile(a_k_norm, (1, 2))], axis=1).reshape(DEPTH, 1, 384)
    seg = jnp.asarray(_segment_mean_matrix(), BF16)
    lamp = jnp.stack([b_lq1, b_lk1, b_lq2, b_lk2], axis=1)
    lamp = jnp.pad(lamp, ((0, 0), (0, SUBLANES - 4), (0, LANES - B_QK_DIM)))
    subln = jnp.tile(b_subln, (1, 2)).reshape(DEPTH, 1, LANES)
    eye = jnp.eye(4, dtype=F32)
    cw = (eye[None, :, None, :, None] * c_w[:, :, :, None, :]).reshape(DEPTH, 256, 256).astype(BF16)
    cscale = c_scale.reshape(DEPTH, 1, GROUP_W)
    gv = d_v_norm.reshape(DEPTH, 1, GROUP_W)
    ws = d_ws.transpose(0, 2, 1, 3).reshape(DEPTH, CHUNK, 4 * CHUNK).astype(BF16)
    bs = jnp.repeat(d_bs.transpose(0, 2, 1), 64, axis=2)
    gpre = g_pre_mix.reshape(DEPTH, 1, D_MODEL)
    gpm = g_post_mix.reshape(DEPTH, 1, D_MODEL)
    gpf = g_pre_ffn.reshape(DEPTH, 1, D_MODEL)
    gpo = g_post_ffn.reshape(DEPTH, 1, D_MODEL)
    rope = tuple(jnp.asarray(t) for t in _rope_tables(dec_seq))
    icnt_ctx = jnp.asarray(_inv_count_table(seq))
    icnt_lat = jnp.asarray(_inv_count_table(dec_seq))
    flat = lambda a, w: a.reshape(dec_batch, DEPTH, past, w).astype(BF16)
    ones = jnp.ones((dec_batch, DEPTH, past, 128), BF16)
    cbv = flat(cache_b_v, 256)
    ctx = (flat(cache_a_k, 128), jnp.concatenate([flat(cache_a_v, 128), ones], axis=-1), flat(cache_b_k, 256),
           jnp.concatenate([cbv[..., 0:128], ones, cbv[..., 128:256], ones], axis=-1))

    cond8 = jnp.concatenate([c_ctx[None, :], c, jnp.zeros((SUBLANES - 1 - dec_batch, D_MODEL), F32)], axis=0)
    mods = _modulation(cond8, w_mod, b_mod)

    yp = x_prompt.reshape(batch * seq, D_MODEL)
    ys = x_sample.reshape(dec_batch * dec_seq, D_MODEL)
    own = [jnp.zeros((batch, DEPTH, seq, w), F32) for w in OWN_WIDTHS]
    for layer in range(DEPTH):
        lam_init = 0.8 - 0.6 * math.exp(-0.3 * layer)
        mix_args = (lamp, subln, cw, cscale, gv, ws, bs, layer)
        ffn_args = (mods, layer, wout, gpm, gpf, gpo, wg, wu, wd)
        qkv, cd, *own = _inproj(yp, mods, layer, gpre, win, gqk, seg, None, own, latent=False, seq=seq)
        mixed = _mixers(qkv, cd, None, icnt_ctx, *mix_args, latent=False, seq=seq, lam_init=lam_init)
        yp = _outproj_ffn(mixed, yp, *ffn_args, latent=False, seq=seq)
        qkv, cd = _inproj(ys, mods, layer, gpre, win, gqk, seg, rope, None, latent=True, seq=dec_seq)
        mixed = _mixers(qkv, cd, ctx, icnt_lat, *mix_args, latent=True, seq=dec_seq, lam_init=lam_init)
        ys = _outproj_ffn(mixed, ys, *ffn_args, latent=True, seq=dec_seq)

    new_a_k, new_a_v, new_b_k, new_b_v = own
    lead = (batch, DEPTH, seq)
    return (yp.reshape(batch, seq, D_MODEL), ys.reshape(dec_batch, dec_seq, D_MODEL),
            new_a_k.reshape(lead + (2, 64)), new_a_v.reshape(lead + (2, 64)),
            new_b_k.reshape(lead + (4, 2, 32)), new_b_v.reshape(lead + (4, 64)))
```

```python
import functools
import math

import numpy as np
import jax
import jax.numpy as jnp
from jax import lax
from jax.experimental import pallas as pl
from jax.experimental.pallas import tpu as pltpu

F32 = jnp.float32
BF16 = jnp.bfloat16

D_MODEL = 1024
DEPTH = 4
GRID_W = 64
GROUP_W = 256
A_HEAD_DIM = 64
B_QK_DIM = 32
POOL_WINDOWS = (2, 4, 8, 16)
CHUNK = 128
D_FF = 2816
N_MOD = 6
RMS_EPS = 1e-6
LOG2E = math.log2(math.e)
ROPE_THETA = 10000.0
IN_WIDTH = 2048

LANES = 128
SUBLANES = 8
VMEM_LIMIT_BYTES = 56 * 1024 * 1024

COL_AQ, COL_AK, COL_AV, COL_BQ, COL_BK, COL_BV, COL_CX, COL_DZ = 0, 256, 384, 512, 768, 1024, 1280, 1536
QKV_AQ, QKV_AV, QKV_BQ, QKV_BK, QKV_BV, QKV_AK = 0, 256, 512, 768, 1024, 1536
QKV_WIDTH = 1664
CD_WIDTH = 768
OWN_WIDTHS = (128, 128, 256, 256)
FF_CHUNK = 256
N_FF_CHUNKS = D_FF // FF_CHUNK

TOKEN_TILE = 1024
CONTEXT_TOKEN_TILE = 512
FFN_TOKEN_TILE = 1024
Q_TILE = 512
CONTEXT_SEQS_PER_STEP = 4
MOD_COL_TILE = 1536
HALO = 8

def _rms(x, g):
    ms = jnp.mean(x * x, axis=-1, keepdims=True)
    return x * lax.rsqrt(ms + RMS_EPS) * g


def _dot(a, b):
    return jnp.dot(a, b, preferred_element_type=F32)


def _dot_nt(a, b):
    return lax.dot_general(a, b, (((1,), (1,)), ((), ())), preferred_element_type=F32)


def _pair_heads(c0, c1, half0):
    return jnp.where(half0, c0, c1), pltpu.roll(jnp.where(half0, c1, c0), 64, axis=1)


def _unpair_heads(d0, d1, half0):
    r = pltpu.roll(d1, 64, axis=1)
    return jnp.where(half0, d0, r), jnp.where(half0, r, d0)


def _mod_kernel(cond_ref, w_ref, b_ref, out_ref):
    s = jax.nn.silu(cond_ref[...]).astype(BF16)
    out_ref[...] = _dot(s, w_ref[...].astype(BF16)) + b_ref[...]


def _modulation(cond8, w_mod, b_mod):
    n_col = (N_MOD * D_MODEL) // MOD_COL_TILE
    out = pl.pallas_call(
        _mod_kernel,
        grid=(DEPTH, n_col),
        in_specs=[
            pl.BlockSpec((SUBLANES, D_MODEL), lambda l, j: (0, 0)),
            pl.BlockSpec((None, D_MODEL, MOD_COL_TILE), lambda l, j: (l, 0, j)),
            pl.BlockSpec((None, 1, MOD_COL_TILE), lambda l, j: (l, 0, j)),
        ],
        out_specs=pl.BlockSpec((None, SUBLANES, MOD_COL_TILE), lambda l, j: (l, 0, j)),
        out_shape=jax.ShapeDtypeStruct((DEPTH, SUBLANES, N_MOD * D_MODEL), F32),
        compiler_params=pltpu.CompilerParams(
            dimension_semantics=("arbitrary", "arbitrary"), vmem_limit_bytes=VMEM_LIMIT_BYTES),
        name="modulation",
    )(cond8, w_mod, b_mod.reshape(DEPTH, 1, N_MOD * D_MODEL))
    return out.reshape(DEPTH, SUBLANES, N_MOD, D_MODEL)


def _rope_cols(x, cos, sin, shift, lo_mask):
    fwd = pltpu.roll(x, LANES - shift, axis=1)
    bwd = pltpu.roll(x, shift, axis=1)
    return x * cos + jnp.where(lo_mask, fwd, bwd) * sin


def _inproj_kernel(*refs, latent, n_aliased):
    if latent:
        (x_ref, mod_ref, gpre_ref, win_ref, gqk_ref, seg_ref, cosa_ref, sina_ref, cosb_ref, sinb_ref,
         qkv_ref, cd_ref) = refs
    else:
        (x_ref, mod_ref, gpre_ref, win_ref, gqk_ref, seg_ref) = refs[:6]
        qkv_ref, cd_ref, oak_ref, oav_ref, obk_ref, obv_ref = refs[6 + n_aliased:]
    h = _rms(x_ref[...], gpre_ref[...]) * (1.0 + mod_ref[1:2, :]) + mod_ref[0:1, :]
    p = _dot(h.astype(BF16), win_ref[...])

    pa = p[:, COL_AQ:COL_AV]
    msq = _dot((pa * pa).astype(BF16), seg_ref[...])
    pa = pa * lax.rsqrt(msq + RMS_EPS) * gqk_ref[...]
    pb = p[:, COL_BQ:COL_BV]
    lane = lax.broadcasted_iota(jnp.int32, (1, LANES), 1)
    if not latent:
        new_rows = ((oak_ref, pa[:, COL_AK:COL_AV]), (oav_ref, p[:, COL_AV:COL_BQ]),
                    (obk_ref, p[:, COL_BK:COL_BV]), (obv_ref, p[:, COL_BV:COL_CX]))
        for ref, val in new_rows:
            if n_aliased:
                ref[...] = val.reshape(ref.shape)
            else:
                ref[:, 0] = val.reshape((ref.shape[0],) + ref.shape[2:])
                ref[:, 1:] = jnp.zeros((ref.shape[0], DEPTH - 1) + ref.shape[2:], F32)
        a_cols = [pa[:, c * LANES:(c + 1) * LANES] for c in range(3)]
        b_cols = [pb[:, c * LANES:(c + 1) * LANES] for c in range(4)]
    else:
        lo_a = (lane % 32) < 16
        lo_b = (lane % 16) < 8
        cosa, sina, cosb, sinb = cosa_ref[...], sina_ref[...], cosb_ref[...], sinb_ref[...]
        a_cols = [_rope_cols(pa[:, c * LANES:(c + 1) * LANES], cosa, sina, 16, lo_a) for c in range(3)]
        b_cols = [_rope_cols(pb[:, c * LANES:(c + 1) * LANES], cosb, sinb, 8, lo_b) for c in range(4)]
    a_scale = A_HEAD_DIM ** -0.5 * LOG2E
    b_scale = B_QK_DIM ** -0.5 * LOG2E
    aq0, aq1 = _pair_heads(a_cols[0] * a_scale, a_cols[1] * a_scale, lane < 64)
    qkv_ref[:, QKV_AQ:QKV_AQ + 128] = aq0.astype(BF16)
    qkv_ref[:, QKV_AQ + 128:QKV_AQ + 256] = aq1.astype(BF16)
    qkv_ref[:, QKV_AV:QKV_AV + 128] = p[:, COL_AV:COL_BQ].astype(BF16)
    ones = jnp.ones((p.shape[0], LANES), BF16)
    qkv_ref[:, QKV_AV + 128:QKV_AV + 256] = ones
    qkv_ref[:, QKV_BQ:QKV_BQ + 128] = (b_cols[0] * b_scale).astype(BF16)
    qkv_ref[:, QKV_BQ + 128:QKV_BQ + 256] = (b_cols[1] * b_scale).astype(BF16)
    qkv_ref[:, QKV_BK:QKV_BK + 128] = b_cols[2].astype(BF16)
    qkv_ref[:, QKV_BK + 128:QKV_BK + 256] = b_cols[3].astype(BF16)
    for c in range(2):
        qkv_ref[:, QKV_BV + 256 * c:QKV_BV + 256 * c + 128] = p[:, COL_BV + 128 * c:COL_BV + 128 * (c + 1)].astype(BF16)
        qkv_ref[:, QKV_BV + 256 * c + 128:QKV_BV + 256 * (c + 1)] = ones
    qkv_ref[:, QKV_AK:QKV_AK + 128] = a_cols[2].astype(BF16)
    cd_ref[:, 0:512] = p[:, COL_DZ:IN_WIDTH]
    cd_ref[:, 512:768] = p[:, COL_CX:COL_DZ]


def _inproj(x, mods, layer, gpre, win, gqk, seg, rope, own, *, latent, seq):
    n = x.shape[0]
    tm = TOKEN_TILE if latent else CONTEXT_TOKEN_TILE
    tiles_per_seq = seq // tm if latent else 1
    if latent:
        mod_row = lambda i: 1 + i // tiles_per_seq
    else:
        mod_row = lambda i: 0
    in_specs = [
        pl.BlockSpec((tm, D_MODEL), lambda i: (i, 0)),
        pl.BlockSpec((None, None, N_MOD, D_MODEL), lambda i: (layer, mod_row(i), 0, 0)),
        pl.BlockSpec((None, 1, D_MODEL), lambda i: (layer, 0, 0)),
        pl.BlockSpec((None, D_MODEL, IN_WIDTH), lambda i: (layer, 0, 0), pipeline_mode=pl.Buffered(1)),
        pl.BlockSpec((None, 1, 384), lambda i: (layer, 0, 0)),
        pl.BlockSpec((384, 384), lambda i: (0, 0)),
    ]
    args = [x, mods, gpre, win, gqk, seg]
    out_specs = [pl.BlockSpec((tm, QKV_WIDTH), lambda i: (i, 0)),
                 pl.BlockSpec((tm, CD_WIDTH), lambda i: (i, 0))]
    out_shape = [jax.ShapeDtypeStruct((n, QKV_WIDTH), BF16), jax.ShapeDtypeStruct((n, CD_WIDTH), F32)]
    aliases = {}
    if latent:
        in_specs += [pl.BlockSpec((tm, LANES), lambda i: (i % tiles_per_seq, 0))] * 4
        args += list(rope)
    else:
        for k, width in enumerate(OWN_WIDTHS):
            if own is None:
                assert layer == 0
                out_specs.append(pl.BlockSpec((tm // seq, DEPTH, seq, width), lambda i: (i, 0, 0, 0)))
            else:
                aliases[len(args)] = len(out_shape)
                in_specs.append(pl.BlockSpec(memory_space=pl.ANY))
                args.append(own[k])
                out_specs.append(pl.BlockSpec((tm // seq, None, seq, width), lambda i: (i, layer, 0, 0)))
            out_shape.append(jax.ShapeDtypeStruct((n // seq, DEPTH, seq, width), F32))
    return pl.pallas_call(
        functools.partial(_inproj_kernel, latent=latent, n_aliased=len(aliases)),
        grid=(n // tm,),
        in_specs=in_specs,
        out_specs=out_specs,
        out_shape=out_shape,
        input_output_aliases=aliases,
        compiler_params=pltpu.CompilerParams(
            dimension_semantics=("arbitrary",), vmem_limit_bytes=VMEM_LIMIT_BYTES),
        name="inproj_latent" if latent else "inproj_context",
    )(*args)


def _scores(qm, k_segs):
    return [_dot_nt(qm, k) for k in k_segs]


def _numerators(ss):
    m = functools.reduce(jnp.maximum, [jnp.max(s, axis=-1, keepdims=True) for s in ss])
    return [jnp.exp2(s - m) for s in ss]


def _rowsum(ps):
    return functools.reduce(jnp.add, [jnp.sum(p, axis=-1, keepdims=True) for p in ps])


def _pool_windows(xh, tq):
    n = tq + 2 * HALO
    up = lambda a, k: pltpu.roll(a, n - k, axis=0)
    a2 = xh + up(xh, 1)
    a4 = a2 + up(a2, 2)
    a8 = a4 + up(a4, 4)
    a16 = a8 + up(a8, 8)
    w2 = up(a2, HALO - 1)[0:tq]
    w4 = up(a4, HALO - 2)[0:tq]
    w8 = up(a8, HALO - 4)[0:tq]
    w16 = a16[0:tq]
    return w2, w4, w8, w16


def _mixer_kernel(*refs, latent, lam_init, tq, n_q, n_sub, seq, ahead):
    if latent:
        (aq_ref, ak_ref, av_ref, bq_ref, bk_ref, bv_ref, cak_ref, cav_ref, cbk_ref, cbv_ref,
         cx_ref, cxp_ref, cxn_ref, dz_ref, icnt_ref, lamp_ref, subln_ref, cw_ref, cscale_ref,
         gv_ref, ws_ref, bs_ref, out_ref) = refs
    else:
        (aq_ref, ak_ref, av_ref, bq_ref, bk_ref, bv_ref,
         cx_ref, dz_ref, icnt_ref, lamp_ref, subln_ref, cw_ref, cscale_ref,
         gv_ref, ws_ref, bs_ref, out_ref) = refs
        cak_ref = cav_ref = cbk_ref = cbv_ref = cxp_ref = cxn_ref = None
    i = pl.program_id(1)
    lane = lax.broadcasted_iota(jnp.int32, (1, LANES), 1)
    half0 = lane < 64

    def pool_mixer(r):
        rows = slice(r * tq, (r + 1) * tq)
        cur = cx_ref[rows, :]
        if latent:
            prev = jnp.where(i == 0, 0.0, cxp_ref[...])
            nxt = jnp.where(i == n_q - 1, 0.0, cxn_ref[...])
        else:
            prev = nxt = jnp.zeros((HALO, GROUP_W), F32)
        xh = jnp.concatenate([prev, cur, nxt], axis=0)
        w2, w4, _, _ = _pool_windows(xh[:, 0:LANES], tq)
        _, _, w8, w16 = _pool_windows(xh[:, LANES:2 * LANES], tq)
        sums = jnp.concatenate([jnp.where(half0, w2, w4), jnp.where(half0, w8, w16)], axis=1)
        pooled = sums * icnt_ref[...] - cur
        yc = _dot(pooled.astype(BF16), cw_ref[...]) * cscale_ref[...]
        out_ref[rows, 512:768] = yc.astype(BF16)

    lane2 = lax.broadcasted_iota(jnp.int32, (1, GROUP_W), 1)

    def gate_mixer(n):
        rows = slice(n * CHUNK, (n + 1) * CHUNK)
        z = jax.nn.gelu(dz_ref[rows, :])
        vc = _rms(z[:, GROUP_W:2 * GROUP_W], gv_ref[...])
        stacked = jnp.concatenate(
            [jnp.where((lane2 >= 64 * g) & (lane2 < 64 * (g + 1)), vc, 0.0).astype(BF16) for g in range(4)],
            axis=0)
        sg = _dot(ws_ref[...], stacked) + bs_ref[...]
        out_ref[rows, 768:1024] = (z[:, 0:GROUP_W] * sg).astype(BF16)

    lp = lamp_ref[...]
    lam = (jnp.exp(jnp.sum(lp[0:1] * lp[1:2], axis=-1, keepdims=True))
           - jnp.exp(jnp.sum(lp[2:3] * lp[3:4], axis=-1, keepdims=True)) + lam_init)

    def key_segments(ctx_ref, own_ref, r, cols):
        return ([ctx_ref[:, cols]] if latent else []) + [own_ref[r * seq:(r + 1) * seq, cols]]

    def head_queries(t):
        r, kind, c, u = heads[t]
        q_ref = aq_ref if kind == "a" else bq_ref
        qc = q_ref[r * tq:(r + 1) * tq, c * LANES:(c + 1) * LANES]
        zero = jnp.zeros_like(qc)
        if kind == "a":
            return [jnp.where(half0 if u == 0 else ~half0, qc, zero)]
        sels = [(lane >= 64 * u + 32 * j) & (lane < 64 * u + 32 * j + 32) for j in range(2)]
        return [jnp.where(sel, qc, zero) for sel in sels]

    def head_scores(t):
        r, kind, c, _ = heads[t]
        if kind == "a":
            k_segs = key_segments(cak_ref, ak_ref, r, slice(0, LANES))
        else:
            k_segs = key_segments(cbk_ref, bk_ref, r, slice(c * LANES, (c + 1) * LANES))
        return [_scores(qm, k_segs) for qm in head_queries(t)]

    def head_weights(t, score_sets):
        kind = heads[t][1]
        if kind == "a":
            return [p.astype(BF16) for p in _numerators(score_sets[0])], None
        return [[p.astype(BF16) for p in _numerators(ss)] for ss in score_sets], None

    def head_output(t, weights):
        r, kind, c, u = heads[t]
        ws, l0 = weights
        if kind == "a":
            a_v = key_segments(cav_ref, av_ref, r, slice(0, 2 * LANES))
            o = functools.reduce(jnp.add, [_dot(w, v) for w, v in zip(ws, a_v)])
            return o[:, 0:LANES] / o[:, LANES:2 * LANES]
        b_v = key_segments(cbv_ref, bv_ref, r, slice(2 * c * LANES, 2 * (c + 1) * LANES))
        o0, o1 = [functools.reduce(jnp.add, [_dot(w, v) for w, v in zip(wj, b_v)]) for wj in ws]
        o = o0[:, 0:LANES] / o0[:, LANES:2 * LANES] - lam * (o1[:, 0:LANES] / o1[:, LANES:2 * LANES])
        valid = half0 if u == 0 else ~half0
        ms = jnp.sum(jnp.where(valid, o * o, 0.0), axis=-1, keepdims=True) * (1.0 / 64)
        return o * lax.rsqrt(ms + RMS_EPS)

    heads = [(r, kind, c, u) for r in range(n_sub) for c in range(2) for u in range(2) for kind in ("a", "b")]
    side_work = ([functools.partial(pool_mixer, r) for r in range(n_sub)]
                 + [functools.partial(gate_mixer, n) for n in range(n_sub * tq // CHUNK)])
    outs = {}
    if ahead is None:
        scores = [head_scores(t) for t in range(len(heads))]
        for work in side_work:
            work()
        weights = [head_weights(t, scores[t]) for t in range(len(heads))]
        for t in range(len(heads)):
            outs[heads[t]] = head_output(t, weights[t])
    else:
        pending = [head_scores(t) for t in range(min(ahead, len(heads)))]
        for t in range(len(heads)):
            if t + ahead < len(heads):
                pending.append(head_scores(t + ahead))
            if side_work:
                side_work.pop(0)()
            outs[heads[t]] = head_output(t, head_weights(t, pending.pop(0)))
        for work in side_work:
            work()
    b_gain = subln_ref[...] * (1.0 - lam_init)
    for r in range(n_sub):
        rows = slice(r * tq, (r + 1) * tq)
        col = lambda kind, c: jnp.where(half0, outs[(r, kind, c, 0)], outs[(r, kind, c, 1)])
        a0, a1 = _unpair_heads(col("a", 0), col("a", 1), half0)
        out_ref[rows, 0:128] = a0.astype(BF16)
        out_ref[rows, 128:256] = a1.astype(BF16)
        out_ref[rows, 256:384] = (col("b", 0) * b_gain).astype(BF16)
        out_ref[rows, 384:512] = (col("b", 1) * b_gain).astype(BF16)


def _mixers(qkv, cd, ctx, icnt, lamp, subln, cw, cscale, gv, ws, bs, layer, *, latent, seq, lam_init):
    n = qkv.shape[0]
    n_batch = n // seq
    tq = min(Q_TILE, seq)
    n_q = seq // tq
    n_sub = 1 if latent else CONTEXT_SEQS_PER_STEP
    rows_q = n_sub * tq
    rows_k = n_sub * seq
    row = lambda b, i: b * n_q + i
    row8 = tq // SUBLANES
    last8 = n // SUBLANES - 1
    in_specs = [
        pl.BlockSpec((rows_q, 256), lambda b, i: (row(b, i), QKV_AQ // 256)),
        pl.BlockSpec((rows_k, 128), lambda b, i: (b, QKV_AK // 128)),
        pl.BlockSpec((rows_k, 256), lambda b, i: (b, QKV_AV // 256)),
        pl.BlockSpec((rows_q, 256), lambda b, i: (row(b, i), QKV_BQ // 256)),
        pl.BlockSpec((rows_k, 256), lambda b, i: (b, QKV_BK // 256)),
        pl.BlockSpec((rows_k, 512), lambda b, i: (b, QKV_BV // 512)),
    ]
    args = [qkv] * 6
    if latent:
        for w in (128, 256, 256, 512):
            in_specs.append(pl.BlockSpec((None, None, 256, w), lambda b, i: (b, layer, 0, 0)))
        args += list(ctx)
    in_specs.append(pl.BlockSpec((rows_q, 256), lambda b, i: (row(b, i), 2)))
    args.append(cd)
    if latent:
        in_specs += [
            pl.BlockSpec((HALO, 256), lambda b, i: (jnp.maximum(row(b, i) * row8 - 1, 0), 2)),
            pl.BlockSpec((HALO, 256), lambda b, i: (jnp.minimum((row(b, i) + 1) * row8, last8), 2)),
        ]
        args += [cd, cd]
    in_specs += [
        pl.BlockSpec((rows_q, 512), lambda b, i: (row(b, i), 0)),
        pl.BlockSpec((tq, 256), lambda b, i: (i, 0)),
        pl.BlockSpec((None, SUBLANES, LANES), lambda b, i: (layer, 0, 0)),
        pl.BlockSpec((None, 1, LANES), lambda b, i: (layer, 0, 0)),
        pl.BlockSpec((None, 256, 256), lambda b, i: (layer, 0, 0)),
        pl.BlockSpec((None, 1, 256), lambda b, i: (layer, 0, 0)),
        pl.BlockSpec((None, 1, 256), lambda b, i: (layer, 0, 0)),
        pl.BlockSpec((None, CHUNK, 4 * CHUNK), lambda b, i: (layer, 0, 0)),
        pl.BlockSpec((None, CHUNK, 256), lambda b, i: (layer, 0, 0)),
    ]
    args += [cd, icnt, lamp, subln, cw, cscale, gv, ws, bs]
    return pl.pallas_call(
        functools.partial(_mixer_kernel, latent=latent, lam_init=lam_init, tq=tq, n_q=n_q, n_sub=n_sub, seq=seq,
                          ahead=1 if latent else None),
        grid=(n_batch // n_sub, n_q),
        in_specs=in_specs,
        out_specs=pl.BlockSpec((rows_q, D_MODEL), lambda b, i: (row(b, i), 0)),
        out_shape=jax.ShapeDtypeStruct((n, D_MODEL), BF16),
        compiler_params=pltpu.CompilerParams(
            dimension_semantics=("arbitrary", "arbitrary"), vmem_limit_bytes=VMEM_LIMIT_BYTES),
        name="mixers_latent" if latent else "mixers_context",
    )(*args)


def _ffn_kernel(mix_ref, x_ref, mod_ref, wout_ref, gpm_ref, gpf_ref, gpo_ref, wg_ref, wu_ref, wd_ref,
                out_ref, acc_ref):
    y = _dot(mix_ref[...], wout_ref[...])
    x1 = x_ref[...] + mod_ref[2:3, :] * _rms(y, gpm_ref[...])
    h = (_rms(x1, gpf_ref[...]) * (1.0 + mod_ref[4:5, :]) + mod_ref[3:4, :]).astype(BF16)
    for c in range(N_FF_CHUNKS):
        cols = slice(c * FF_CHUNK, (c + 1) * FF_CHUNK)
        g = _dot(h, wg_ref[:, cols])
        u = _dot(h, wu_ref[:, cols])
        part = _dot((jax.nn.silu(g) * u).astype(BF16), wd_ref[cols, :])
        if c == 0:
            acc_ref[...] = part
        else:
            acc_ref[...] += part
    out_ref[...] = x1 + mod_ref[5:6, :] * _rms(acc_ref[...], gpo_ref[...])


def _outproj_ffn(mixed, x, mods, layer, wout, gpm, gpf, gpo, wg, wu, wd, *, latent, seq):
    n = x.shape[0]
    tm = FFN_TOKEN_TILE
    tiles_per_seq = seq // tm if latent else 1
    if latent:
        mod_row = lambda i: 1 + i // tiles_per_seq
    else:
        mod_row = lambda i: 0
    const = dict(pipeline_mode=pl.Buffered(1))
    gain = lambda: pl.BlockSpec((None, 1, D_MODEL), lambda i: (layer, 0, 0))
    return pl.pallas_call(
        _ffn_kernel,
        grid=(n // tm,),
        in_specs=[
            pl.BlockSpec((tm, D_MODEL), lambda i: (i, 0)),
            pl.BlockSpec((tm, D_MODEL), lambda i: (i, 0)),
            pl.BlockSpec((None, None, N_MOD, D_MODEL), lambda i: (layer, mod_row(i), 0, 0)),
            pl.BlockSpec((None, D_MODEL, D_MODEL), lambda i: (layer, 0, 0), **const),
            gain(), gain(), gain(),
            pl.BlockSpec((None, D_MODEL, D_FF), lambda i: (layer, 0, 0), **const),
            pl.BlockSpec((None, D_MODEL, D_FF), lambda i: (layer, 0, 0), **const),
            pl.BlockSpec((None, D_FF, D_MODEL), lambda i: (layer, 0, 0), **const),
        ],
        out_specs=pl.BlockSpec((tm, D_MODEL), lambda i: (i, 0)),
        out_shape=jax.ShapeDtypeStruct((n, D_MODEL), F32),
        scratch_shapes=[pltpu.VMEM((tm, D_MODEL), F32)],
        compiler_params=pltpu.CompilerParams(
            dimension_semantics=("arbitrary",), vmem_limit_bytes=VMEM_LIMIT_BYTES),
        name="outproj_ffn_latent" if latent else "outproj_ffn_context",
    )(mixed, x, mods, wout, gpm, gpf, gpo, wg, wu, wd)


def _rope_tables(seq):
    t = np.arange(seq)
    pos = {"row": (t // GRID_W).astype(np.float64), "col": (t % GRID_W).astype(np.float64)}

    def table(head_dim):
        half = head_dim // 2
        inv_freq = ROPE_THETA ** (-np.arange(0, half, 2, dtype=np.float64) / half)
        lane = np.arange(head_dim)
        within = lane % half
        freq = inv_freq[within % (half // 2)]
        p = np.where(lane[None, :] < half, pos["row"][:, None], pos["col"][:, None])
        ang = p * freq[None, :]
        sign = np.where(within < half // 2, -1.0, 1.0)
        reps = LANES // head_dim
        return (np.tile(np.cos(ang), (1, reps)).astype(np.float32),
                np.tile(np.sin(ang) * sign[None, :], (1, reps)).astype(np.float32))

    cosa, sina = table(A_HEAD_DIM)
    cosb, sinb = table(B_QK_DIM)
    return cosa, sina, cosb, sinb


def _inv_count_table(seq):
    t = np.arange(seq)
    cols = []
    for win in POOL_WINDOWS:
        cnt = np.minimum(t + win // 2, seq) - np.maximum(t - win // 2, 0)
        cols.append(np.repeat((1.0 / cnt)[:, None], 64, axis=1))
    return np.concatenate(cols, axis=1).astype(np.float32)


def _segment_mean_matrix():
    idx = np.arange(384) // A_HEAD_DIM
    return (idx[:, None] == idx[None, :]).astype(np.float32) / A_HEAD_DIM


def kernel(x_prompt, x_sample, cache_a_k, cache_a_v, cache_b_k, cache_b_v, c, c_ctx, w_mod, b_mod, g_pre_mix, g_post_mix, g_pre_ffn, g_post_ffn, w_in, w_out, a_q_norm, a_k_norm, b_lq1, b_lk1, b_lq2, b_lk2, b_subln, c_w, c_scale, d_v_norm, d_ws, d_bs, w_gate, w_up, w_down):
    batch, seq, _ = x_prompt.shape
    dec_batch, dec_seq, _ = x_sample.shape
    past = cache_a_k.shape[2]

    win = w_in.astype(BF16)
    wout = w_out.astype(BF16)
    wg = w_gate.astype(BF16)
    wu = w_up.astype(BF16)
    wd = w_down.astype(BF16)
    gqk = jnp.concatenate([jnp.tile(a_q_norm, (1, 4)), jnp.tile(a_k_norm, (1, 2))], axis=1).reshape(DEPTH, 1, 384)
    seg = jnp.asarray(_segment_mean_matrix(), BF16)
    lamp = jnp.stack([b_lq1, b_lk1, b_lq2, b_lk2], axis=1)
    lamp = jnp.pad(lamp, ((0, 0), (0, SUBLANES - 4), (0, LANES - B_QK_DIM)))
    subln = jnp.tile(b_subln, (1, 2)).reshape(DEPTH, 1, LANES)
    eye = jnp.eye(4, dtype=F32)
    cw = (eye[None, :, None, :, None] * c_w[:, :, :, None, :]).reshape(DEPTH, 256, 256).astype(BF16)
    cscale = c_scale.reshape(DEPTH, 1, GROUP_W)
    gv = d_v_norm.reshape(DEPTH, 1, GROUP_W)
    ws = d_ws.transpose(0, 2, 1, 3).reshape(DEPTH, CHUNK, 4 * CHUNK).astype(BF16)
    bs = jnp.repeat(d_bs.transpose(0, 2, 1), 64, axis=2)
    gpre = g_pre_mix.reshape(DEPTH, 1, D_MODEL)
    gpm = g_post_mix.reshape(DEPTH, 1, D_MODEL)
    gpf = g_pre_ffn.reshape(DEPTH, 1, D_MODEL)
    gpo = g_post_ffn.reshape(DEPTH, 1, D_MODEL)
    rope = tuple(jnp.asarray(t) for t in _rope_tables(dec_seq))
    icnt_ctx = jnp.asarray(_inv_count_table(seq))
    icnt_lat = jnp.asarray(_inv_count_table(dec_seq))
    flat = lambda a, w: a.reshape(dec_batch, DEPTH, past, w).astype(BF16)
    ones = jnp.ones((dec_batch, DEPTH, past, 128), BF16)
    cbv = flat(cache_b_v, 256)
    ctx = (flat(cache_a_k, 128), jnp.concatenate([flat(cache_a_v, 128), ones], axis=-1), flat(cache_b_k, 256),
           jnp.concatenate([cbv[..., 0:128], ones, cbv[..., 128:256], ones], axis=-1))

    cond8 = jnp.concatenate([c_ctx[None, :], c, jnp.zeros((SUBLANES - 1 - dec_batch, D_MODEL), F32)], axis=0)
    mods = _modulation(cond8, w_mod, b_mod)

    yp = x_prompt.reshape(batch * seq, D_MODEL)
    ys = x_sample.reshape(dec_batch * dec_seq, D_MODEL)
    own = None
    for layer in range(DEPTH):
        lam_init = 0.8 - 0.6 * math.exp(-0.3 * layer)
        mix_args = (lamp, subln, cw, cscale, gv, ws, bs, layer)
        ffn_args = (mods, layer, wout, gpm, gpf, gpo, wg, wu, wd)
        qkv, cd, *own = _inproj(yp, mods, layer, gpre, win, gqk, seg, None, own, latent=False, seq=seq)
        mixed = _mixers(qkv, cd, None, icnt_ctx, *mix_args, latent=False, seq=seq, lam_init=lam_init)
        yp = _outproj_ffn(mixed, yp, *ffn_args, latent=False, seq=seq)
        qkv, cd = _inproj(ys, mods, layer, gpre, win, gqk, seg, rope, None, latent=True, seq=dec_seq)
        mixed = _mixers(qkv, cd, ctx, icnt_lat, *mix_args, latent=True, seq=dec_seq, lam_init=lam_init)
        ys = _outproj_ffn(mixed, ys, *ffn_args, latent=True, seq=dec_seq)

    new_a_k, new_a_v, new_b_k, new_b_v = own
    lead = (batch, DEPTH, seq)
    return (yp.reshape(batch, seq, D_MODEL), ys.reshape(dec_batch, dec_seq, D_MODEL),
            new_a_k.reshape(lead + (2, 64)), new_a_v.reshape(lead + (2, 64)),
            new_b_k.reshape(lead + (4, 2, 32)), new_b_v.reshape(lead + (4, 64)))
```
